```python
import jax, jax.numpy as jnp
from jax import lax
import numpy as np

D_MODEL = 4096
BATCH = 4
SEQ = 2048
DEPTH = 1
DEC_BATCH = 128
DEC_SEQ = 1
PAST_LEN = 16384
PAGE_SIZE = 128

D_CONV = D_MODEL // 2
CONV_W = 3
GLA_HEADS = 8
GLA_DK = 256
GLA_DV = 256
GLA_KEY = GLA_HEADS * GLA_DK
GLA_VAL = GLA_HEADS * GLA_DV
GATE_RANK = 16
GATE_NORM = 16.0
GLA_CHUNK = 32
D_FF = 11008
PLE_DIM = 256
EPS = 1e-6

kernel_name = 'hybrid_shortconv_gla_convffn_step'


def _split_sizes():
    return [D_CONV, D_CONV, D_CONV, GLA_KEY, GLA_KEY, GLA_VAL, GLA_VAL, GATE_RANK, D_MODEL, D_MODEL]


def _rmsnorm(x, g):
    xf = x.astype(jnp.float32)
    y = xf * lax.rsqrt(jnp.mean(xf * xf, axis=-1, keepdims=True) + EPS)
    return (y * g.astype(jnp.float32)).astype(x.dtype)


def _causal_conv(u, buf, w):
    T = u.shape[1]
    up = jnp.concatenate([buf.astype(u.dtype), u], axis=1)
    out = w[0] * up[:, 0:T]
    for j in range(1, CONV_W):
        out = out + w[j] * up[:, j:j + T]
    return out, up[:, -(CONV_W - 1):]


def _gla(q, k, v, log_a, S0):
    Bsz, T, H, DK = q.shape
    DV = v.shape[-1]
    C = GLA_CHUNK
    nC = -(-T // C)
    pad = nC * C - T
    f32 = jnp.float32
    q, k, v, log_a = (t.astype(f32) for t in (q, k, v, log_a))
    if pad:
        pw = ((0, 0), (0, pad), (0, 0), (0, 0))
        q, k, v, log_a = (jnp.pad(t, pw) for t in (q, k, v, log_a))
    q = q.reshape(Bsz, nC, C, H, DK)
    k = k.reshape(Bsz, nC, C, H, DK)
    v = v.reshape(Bsz, nC, C, H, DV)
    b = jnp.cumsum(log_a.reshape(Bsz, nC, C, H, DK), axis=2)
    b_last = b[:, :, -1:]
    q_t = q * jnp.exp(b)
    k_t = k * jnp.exp(-b)
    k_s = k * jnp.exp(b_last - b)
    mask = jnp.tril(jnp.ones((C, C), dtype=bool))
    scores = jnp.einsum('bnthk,bnshk->bnhts', q_t, k_t)
    scores = jnp.where(mask, scores, 0.0)
    o_intra = jnp.einsum('bnhts,bnshv->bnthv', scores, v)
    decay = jnp.exp(b_last[:, :, 0])

    def step(S, inp):
        qc, kc, vc, dc = inp
        o = jnp.einsum('bthk,bhkv->bthv', qc, S)
        S = S * dc[..., None] + jnp.einsum('bthk,bthv->bhkv', kc, vc)
        return S, o

    xs = (jnp.moveaxis(q_t, 1, 0), jnp.moveaxis(k_s, 1, 0), jnp.moveaxis(v, 1, 0), jnp.moveaxis(decay, 1, 0))
    S_fin, o_inter = lax.scan(step, S0.astype(f32), xs)
    o = o_intra + jnp.moveaxis(o_inter, 0, 1)
    o = o.reshape(Bsz, nC * C, H, DV)[:, :T]
    return o, S_fin


def _layer(h, p_i, conv_buf, gla_S, ffn_buf, lw):
    (g_mix, w_in, w_alpha2, b_alpha, w_conv, w_out_conv, g_gla, w_out_gla, w_mix_out,
     g_ffn, w_up, w_ffn_conv, w_down, g_ple, w_ple_gate, w_ple_proj) = lw
    Bsz, T, _ = h.shape
    n = _rmsnorm(h, g_mix)
    z = n @ w_in
    idx = [int(s) for s in np.cumsum(_split_sizes())[:-1]]
    hA, cA, bA, q, k, v, r, a_lr, gate_a, gate_b = jnp.split(z, idx, axis=-1)
    cu, conv_new = _causal_conv(cA * hA, conv_buf, w_conv)
    yA = (bA * cu) @ w_out_conv
    log_a = jax.nn.log_sigmoid((a_lr @ w_alpha2 + b_alpha).astype(jnp.float32)) / GATE_NORM
    q = q.reshape(Bsz, T, GLA_HEADS, GLA_DK) * (GLA_DK ** -0.5)
    k = k.reshape(Bsz, T, GLA_HEADS, GLA_DK)
    v = v.reshape(Bsz, T, GLA_HEADS, GLA_DV)
    log_a = log_a.reshape(Bsz, T, GLA_HEADS, GLA_DK)
    o, S_new = _gla(q, k, v, log_a, gla_S)
    o = _rmsnorm(o.astype(h.dtype), g_gla).reshape(Bsz, T, GLA_VAL)
    yB = (o * jax.nn.silu(r)) @ w_out_gla
    merged = jax.nn.sigmoid(gate_a) * yA + jax.nn.sigmoid(gate_b) * yB
    h = h + merged @ w_mix_out
    up, ffn_new = _causal_conv(_rmsnorm(h, g_ffn) @ w_up, ffn_buf, w_ffn_conv)
    fg, fv = jnp.split(up, 2, axis=-1)
    h = h + (jax.nn.silu(fg) * fv) @ w_down
    h = h + jax.nn.sigmoid(_rmsnorm(h, g_ple) @ w_ple_gate) * (p_i @ w_ple_proj)
    return h, conv_new.astype(conv_buf.dtype), S_new.astype(gla_S.dtype), ffn_new.astype(ffn_buf.dtype)


def _trunk(x, p, conv_states, gla_states, ffn_states, weights, g_final):
    h = x
    convs, glas, ffns = [], [], []
    for i in range(DEPTH):
        lw = tuple(w[i] for w in weights)
        h, c_new, s_new, f_new = _layer(h, p[i], conv_states[i], gla_states[i], ffn_states[i], lw)
        convs.append(c_new)
        glas.append(s_new)
        ffns.append(f_new)
    return _rmsnorm(h, g_final), jnp.stack(convs), jnp.stack(glas), jnp.stack(ffns)


def setup_inputs(seed: int = 0) -> dict:
    key = jax.random.key(seed)
    ks = jax.random.split(key, 32)
    nrm = jax.random.normal
    f32 = jnp.float32
    d_in = sum(_split_sizes())

    def w(k, shape, fan_in):
        return nrm(k, shape, f32) * (fan_in ** -0.5)

    def gain(k, shape):
        return 1.0 + 0.01 * nrm(k, shape, f32)

    return {
        'x_prompt': nrm(ks[0], (BATCH, SEQ, D_MODEL), f32),
        'x_sample': nrm(ks[1], (DEC_BATCH, DEC_SEQ, D_MODEL), f32),
        'p_prompt': nrm(ks[2], (DEPTH, BATCH, SEQ, PLE_DIM), f32),
        'p_sample': nrm(ks[3], (DEPTH, DEC_BATCH, DEC_SEQ, PLE_DIM), f32),
        'state_conv': nrm(ks[4], (DEPTH, DEC_BATCH, CONV_W - 1, D_CONV), f32),
        'state_gla': 0.5 * nrm(ks[5], (DEPTH, DEC_BATCH, GLA_HEADS, GLA_DK, GLA_DV), f32),
        'state_ffn': nrm(ks[6], (DEPTH, DEC_BATCH, CONV_W - 1, 2 * D_FF), f32),
        'g_mix': gain(ks[7], (DEPTH, D_MODEL)),
        'w_in': w(ks[8], (DEPTH, D_MODEL, d_in), D_MODEL),
        'w_alpha2': w(ks[9], (DEPTH, GATE_RANK, GLA_KEY), GATE_RANK),
        'b_alpha': 0.01 * nrm(ks[10], (DEPTH, GLA_KEY), f32),
        'w_conv': w(ks[11], (DEPTH, CONV_W, D_CONV), CONV_W),
        'w_out_conv': w(ks[12], (DEPTH, D_CONV, D_MODEL), D_CONV),
        'g_gla': gain(ks[13], (DEPTH, GLA_DV)),
        'w_out_gla': w(ks[14], (DEPTH, GLA_VAL, D_MODEL), GLA_VAL),
        'w_mix_out': w(ks[15], (DEPTH, D_MODEL, D_MODEL), D_MODEL),
        'g_ffn': gain(ks[16], (DEPTH, D_MODEL)),
        'w_up': w(ks[17], (DEPTH, D_MODEL, 2 * D_FF), D_MODEL),
        'w_ffn_conv': w(ks[18], (DEPTH, CONV_W, 2 * D_FF), CONV_W),
        'w_down': w(ks[19], (DEPTH, D_FF, D_MODEL), D_FF),
        'g_ple': gain(ks[20], (DEPTH, D_MODEL)),
        'w_ple_gate': w(ks[21], (DEPTH, D_MODEL, D_MODEL), D_MODEL),
        'w_ple_proj': w(ks[22], (DEPTH, PLE_DIM, D_MODEL), PLE_DIM),
        'g_final': gain(ks[23], (D_MODEL,)),
    }


def reference(x_prompt, x_sample, p_prompt, p_sample, state_conv, state_gla, state_ffn,
              g_mix, w_in, w_alpha2, b_alpha, w_conv, w_out_conv, g_gla, w_out_gla, w_mix_out,
              g_ffn, w_up, w_ffn_conv, w_down, g_ple, w_ple_gate, w_ple_proj, g_final):
    weights = (g_mix, w_in, w_alpha2, b_alpha, w_conv, w_out_conv, g_gla, w_out_gla, w_mix_out,
               g_ffn, w_up, w_ffn_conv, w_down, g_ple, w_ple_gate, w_ple_proj)
    Bp = x_prompt.shape[0]
    dt = x_prompt.dtype
    conv0 = jnp.zeros((DEPTH, Bp, CONV_W - 1, D_CONV), dt)
    gla0 = jnp.zeros((DEPTH, Bp, GLA_HEADS, GLA_DK, GLA_DV), dt)
    ffn0 = jnp.zeros((DEPTH, Bp, CONV_W - 1, 2 * D_FF), dt)
    y_prompt, conv_p, gla_p, ffn_p = _trunk(x_prompt, p_prompt, conv0, gla0, ffn0, weights, g_final)
    y_sample, conv_s, gla_s, ffn_s = _trunk(x_sample, p_sample, state_conv, state_gla, state_ffn, weights, g_final)
    return (y_prompt, y_sample, conv_p, conv_s, gla_p, gla_s, ffn_p, ffn_s)
```

```python
import functools

import jax
import jax.numpy as jnp
from jax import lax
from jax.experimental import pallas as pl
from jax.experimental.pallas import tpu as pltpu

F32 = jnp.float32
BF16 = jnp.bfloat16

D_MODEL = 4096
D_CONV = 2048
CONV_W = 3
GLA_HEADS = 8
GLA_DK = 256
GLA_DV = 256
GLA_KEY = GLA_HEADS * GLA_DK
GLA_VAL = GLA_HEADS * GLA_DV
GATE_RANK = 16
GATE_NORM = 16.0
GLA_CHUNK = 32
D_FF = 11008
EPS = 1e-6

V7X_VMEM_BYTES = 64 * 2**20
V7X_VMEM_CAP = 60 * 2**20
SUBLANES = 8
LANES = 128
GLA_SUB = 256

OFF_HA = 0
OFF_CA = D_CONV
OFF_BA = 2 * D_CONV
OFF_QKVR = 3 * D_CONV
OFF_ALR = OFF_QKVR + 2 * GLA_KEY + 2 * GLA_VAL
OFF_GATES = OFF_ALR + GATE_RANK


def _vmem_limit(nbytes):
    return int(min(V7X_VMEM_CAP, nbytes + 8 * 2**20))


def _nbytes(shape, dtype):
    n = 1
    for s in shape:
        n *= s
    return n * jnp.dtype(dtype).itemsize


def _rmsnorm_kernel(x_ref, g_ref, o_ref):
    x = x_ref[...]
    ms = jnp.mean(x * x, axis=-1, keepdims=True)
    o_ref[...] = ((x * lax.rsqrt(ms + EPS)) * g_ref[...]).astype(o_ref.dtype)


def _rmsnorm(x, g, out_dtype, tm):
    m, d = x.shape
    return pl.pallas_call(
        _rmsnorm_kernel,
        grid=(m // tm,),
        in_specs=[pl.BlockSpec((tm, d), lambda i: (i, 0)),
                  pl.BlockSpec((1, d), lambda i: (0, 0))],
        out_specs=pl.BlockSpec((tm, d), lambda i: (i, 0)),
        out_shape=jax.ShapeDtypeStruct((m, d), out_dtype),
        compiler_params=pltpu.CompilerParams(
            dimension_semantics=("arbitrary",),
            vmem_limit_bytes=_vmem_limit(6 * tm * d * 4)),
        name="rmsnorm",
    )(x, g.reshape(1, d))


def _cast_rows(src_ref, dst_ref):
    k = src_ref.shape[0]
    ck = 256
    assert k % ck == 0

    def body(c, carry):
        r = pl.multiple_of(c * ck, ck)
        dst_ref[pl.ds(r, ck), :] = src_ref[pl.ds(r, ck), :].astype(BF16)
        return carry

    lax.fori_loop(0, k // ck, body, 0)


def _fused_matmul(name, a_list, w_list, extra_list, out_list, epilogue, *, tm, tn, nj,
                  scratch=()):
    m = a_list[0].shape[0]
    assert m % tm == 0
    ni = m // tm
    na, nw, ne, no = len(a_list), len(w_list), len(extra_list), len(out_list)
    cast_ids = [n for n, (w, _, _) in enumerate(w_list) if w.dtype != BF16]

    in_specs = [pl.BlockSpec((tm, a.shape[1]), lambda j, i: (i, 0)) for a in a_list]
    for w, _, off in w_list:
        in_specs.append(pl.BlockSpec((w.shape[0], tn),
                                     functools.partial(lambda j, i, off: (0, off + j), off=off)))
    in_specs += [pl.BlockSpec(blk, im) for _, blk, im in extra_list]
    out_specs = [pl.BlockSpec(blk, im) for _, blk, im in out_list]
    out_shape = [s for s, _, _ in out_list]
    scratch_shapes = [pltpu.VMEM((w_list[n][0].shape[0], tn), BF16) for n in cast_ids]
    scratch_shapes += list(scratch)

    def kernel(*refs):
        a_refs = refs[:na]
        w_refs = refs[na:na + nw]
        e_refs = refs[na + nw:na + nw + ne]
        o_refs = refs[na + nw + ne:na + nw + ne + no]
        s_refs = refs[na + nw + ne + no:]
        wb_refs = dict(zip(cast_ids, s_refs[:len(cast_ids)]))
        user_scratch = s_refs[len(cast_ids):]
        j = pl.program_id(0)
        i = pl.program_id(1)

        @pl.when(i == 0)
        def _():
            for n in cast_ids:
                _cast_rows(w_refs[n], wb_refs[n])

        accs = []
        for n, (_, ai, _) in enumerate(w_list):
            w_ref = wb_refs[n] if n in wb_refs else w_refs[n]
            accs.append(jnp.dot(a_refs[ai][...], w_ref[...], preferred_element_type=F32))
        epilogue(accs, e_refs, o_refs, user_scratch, j, i)

    est = 0
    for a in a_list:
        est += 2 * _nbytes((tm, a.shape[1]), a.dtype)
    for w, _, _ in w_list:
        est += 2 * _nbytes((w.shape[0], tn), w.dtype)
        if w.dtype != BF16:
            est += _nbytes((w.shape[0], tn), BF16)
    for arr, blk, _ in list(extra_list) + list(out_list):
        est += 2 * _nbytes(blk, arr.dtype)
    est += 2 * nw * tm * tn * 4

    return pl.pallas_call(
        kernel,
        grid=(nj, ni),
        in_specs=in_specs,
        out_specs=out_specs,
        out_shape=out_shape,
        scratch_shapes=scratch_shapes,
        compiler_params=pltpu.CompilerParams(
            dimension_semantics=("arbitrary", "arbitrary"),
            vmem_limit_bytes=_vmem_limit(est)),
        name=name,
    )(*a_list, *[w for w, _, _ in w_list], *[e for e, _, _ in extra_list])


def _tile(tm, tn):
    return (tm, tn), (lambda j, i: (i, j))


def _conv3_rows(u, prev8, w3):
    tm = u.shape[0]
    row = lax.broadcasted_iota(jnp.int32, u.shape, 0)
    p0 = prev8[SUBLANES - 2:SUBLANES - 1, :]
    p1 = prev8[SUBLANES - 1:SUBLANES, :]
    um1 = jnp.where(row == 0, p1, pltpu.roll(u, 1, axis=0))
    um2 = jnp.where(row == 0, p0, jnp.where(row == 1, p1, pltpu.roll(u, 2, axis=0)))
    del tm
    return w3[0:1, :] * um2 + w3[1:2, :] * um1 + w3[2:3, :] * u


def _silu(x):
    return x * jax.nn.sigmoid(x)


def _short_conv_branch(n1, w_in, w_conv, state2d, *, seq_len, tm, tn):
    m = n1.shape[0]
    nj = D_CONV // tn
    w_list = [(w_in, 0, OFF_HA // tn), (w_in, 0, OFF_CA // tn), (w_in, 0, OFF_BA // tn)]
    wc = (w_conv, (CONV_W, tn), lambda j, i: (0, j))
    if seq_len > 1:
        tps = seq_len // tm
        n_seq = m // seq_len

        def epilogue(accs, e_refs, o_refs, s_refs, j, i):
            h, c, b = accs
            (wc_ref,), (xa_ref, last_ref), (carry,) = e_refs, o_refs, s_refs
            u = c * h

            @pl.when(i % tps == 0)
            def _():
                carry[...] = jnp.zeros_like(carry)

            cu = _conv3_rows(u, carry[...], wc_ref[...])
            xa_ref[...] = (b * cu).astype(BF16)
            last8 = u[tm - SUBLANES:tm, :]
            carry[...] = last8
            last_ref[0] = last8

        outs = [(jax.ShapeDtypeStruct((m, D_CONV), BF16),) + _tile(tm, tn),
                (jax.ShapeDtypeStruct((n_seq, SUBLANES, D_CONV), F32), (1, SUBLANES, tn),
                 lambda j, i: (i // tps, 0, j))]
        return _fused_matmul("short_conv_seq", [n1], w_list, [wc], outs, epilogue,
                             tm=tm, tn=tn, nj=nj, scratch=[pltpu.VMEM((SUBLANES, tn), F32)])

    def epilogue(accs, e_refs, o_refs, s_refs, j, i):
        h, c, b = accs
        wc_ref, s0_ref, s1_ref = e_refs
        xa_ref, u_ref = o_refs
        u = c * h
        w3 = wc_ref[...]
        cu = w3[0:1, :] * s0_ref[...] + w3[1:2, :] * s1_ref[...] + w3[2:3, :] * u
        xa_ref[...] = (b * cu).astype(BF16)
        u_ref[...] = u

    extras = [wc,
              (state2d, (tm, tn), lambda j, i: (i, j)),
              (state2d, (tm, tn), lambda j, i: (i, D_CONV // tn + j))]
    outs = [(jax.ShapeDtypeStruct((m, D_CONV), BF16),) + _tile(tm, tn),
            (jax.ShapeDtypeStruct((m, D_CONV), F32),) + _tile(tm, tn)]
    return _fused_matmul("short_conv_step", [n1], w_list, extras, outs, epilogue,
                         tm=tm, tn=tn, nj=nj)


def _qkvr_proj(n1, w_in, *, tm, tn):
    m = n1.shape[0]
    n = 2 * GLA_KEY + 2 * GLA_VAL

    def epilogue(accs, e_refs, o_refs, s_refs, j, i):
        o_refs[0][...] = accs[0]

    outs = [(jax.ShapeDtypeStruct((m, n), F32),) + _tile(tm, tn)]
    return _fused_matmul("qkvr_proj", [n1], [(w_in, 0, OFF_QKVR // tn)], [], outs, epilogue,
                         tm=tm, tn=tn, nj=n // tn)[0]


def _log_decay_kernel(n_ref, w1_ref, w2_ref, b_ref, o_ref):
    a_lr = jnp.dot(n_ref[...], w1_ref[...], preferred_element_type=F32)
    x = jnp.dot(a_lr.astype(BF16), w2_ref[...], preferred_element_type=F32) + b_ref[...]
    log_sig = jnp.minimum(x, 0.0) - jnp.log1p(jnp.exp(-jnp.abs(x)))
    o_ref[...] = log_sig / GATE_NORM


def _log_decay(n1, w_alr, w_alpha2, b_alpha, *, tm):
    m, d = n1.shape
    w1 = jnp.pad(w_alr, ((0, 0), (0, LANES - GATE_RANK))).astype(BF16)
    w2 = jnp.pad(w_alpha2, ((0, LANES - GATE_RANK), (0, 0))).astype(BF16)
    return pl.pallas_call(
        _log_decay_kernel,
        grid=(m // tm,),
        in_specs=[pl.BlockSpec((tm, d), lambda i: (i, 0)),
                  pl.BlockSpec((d, LANES), lambda i: (0, 0)),
                  pl.BlockSpec((LANES, GLA_KEY), lambda i: (0, 0)),
                  pl.BlockSpec((1, GLA_KEY), lambda i: (0, 0))],
        out_specs=pl.BlockSpec((tm, GLA_KEY), lambda i: (i, 0)),
        out_shape=jax.ShapeDtypeStruct((m, GLA_KEY), F32),
        compiler_params=pltpu.CompilerParams(
            dimension_semantics=("arbitrary",),
            vmem_limit_bytes=_vmem_limit(2 * tm * d * 2 + 6 * tm * GLA_KEY * 4 + 4 * 2**20)),
        name="log_decay",
    )(n1, w1, w2, b_alpha.reshape(1, GLA_KEY))


def _merge(n1, xa, xb, w_gates, w_out_conv, w_out_gla, *, tm, tn):
    m = n1.shape[0]

    def epilogue(accs, e_refs, o_refs, s_refs, j, i):
        ga, gb, ya, yb = accs
        o_refs[0][...] = (jax.nn.sigmoid(ga) * ya + jax.nn.sigmoid(gb) * yb).astype(BF16)

    w_list = [(w_gates, 0, 0), (w_gates, 0, D_MODEL // tn), (w_out_conv, 1, 0), (w_out_gla, 2, 0)]
    outs = [(jax.ShapeDtypeStruct((m, D_MODEL), BF16),) + _tile(tm, tn)]
    return _fused_matmul("gated_merge", [n1, xa, xb], w_list, [], outs, epilogue,
                         tm=tm, tn=tn, nj=D_MODEL // tn)[0]


def _proj_residual(name, a, w, res, *, tm, tn):
    m = a.shape[0]
    n = w.shape[1]

    def epilogue(accs, e_refs, o_refs, s_refs, j, i):
        o_refs[0][...] = e_refs[0][...] + accs[0]

    extras = [(res,) + _tile(tm, tn)]
    outs = [(jax.ShapeDtypeStruct((m, n), F32),) + _tile(tm, tn)]
    return _fused_matmul(name, [a], [(w, 0, 0)], extras, outs, epilogue,
                         tm=tm, tn=tn, nj=n // tn)[0]


def _conv_ffn_up(n2, w_up, w_ffn_conv, state2d, *, seq_len, tm, tn):
    m = n2.shape[0]
    nj = D_FF // tn
    w_list = [(w_up, 0, 0), (w_up, 0, nj)]
    wcg = (w_ffn_conv, (CONV_W, tn), lambda j, i: (0, j))
    wcv = (w_ffn_conv, (CONV_W, tn), lambda j, i: (0, nj + j))
    if seq_len > 1:
        tps = seq_len // tm
        n_seq = m // seq_len

        def epilogue(accs, e_refs, o_refs, s_refs, j, i):
            ug, uv = accs
            wg_ref, wv_ref = e_refs
            act_ref, lastg_ref, lastv_ref = o_refs
            carry_g, carry_v = s_refs

            @pl.when(i % tps == 0)
            def _():
                carry_g[...] = jnp.zeros_like(carry_g)
                carry_v[...] = jnp.zeros_like(carry_v)

            cg = _conv3_rows(ug, carry_g[...], wg_ref[...])
            cv = _conv3_rows(uv, carry_v[...], wv_ref[...])
            act_ref[...] = (_silu(cg) * cv).astype(BF16)
            last_g = ug[tm - SUBLANES:tm, :]
            last_v = uv[tm - SUBLANES:tm, :]
            carry_g[...] = last_g
            carry_v[...] = last_v
            lastg_ref[0] = last_g
            lastv_ref[0] = last_v

        last = (jax.ShapeDtypeStruct((n_seq, SUBLANES, D_FF), F32), (1, SUBLANES, tn),
                lambda j, i: (i // tps, 0, j))
        outs = [(jax.ShapeDtypeStruct((m, D_FF), BF16),) + _tile(tm, tn), last, last]
        return _fused_matmul("conv_ffn_up_seq", [n2], w_list, [wcg, wcv], outs, epilogue,
                             tm=tm, tn=tn, nj=nj,
                             scratch=[pltpu.VMEM((SUBLANES, tn), F32)] * 2)

    def epilogue(accs, e_refs, o_refs, s_refs, j, i):
        ug, uv = accs
        wg_ref, wv_ref, g0, v0, g1, v1 = e_refs
        act_ref, ug_ref, uv_ref = o_refs
        wg = wg_ref[...]
        wv = wv_ref[...]
        cg = wg[0:1, :] * g0[...] + wg[1:2, :] * g1[...] + wg[2:3, :] * ug
        cv = wv[0:1, :] * v0[...] + wv[1:2, :] * v1[...] + wv[2:3, :] * uv
        act_ref[...] = (_silu(cg) * cv).astype(BF16)
        ug_ref[...] = ug
        uv_ref[...] = uv

    def st(off):
        return (state2d, (tm, tn), functools.partial(lambda j, i, off: (i, off + j), off=off))

    extras = [wcg, wcv, st(0), st(nj), st(2 * nj), st(3 * nj)]
    outs = [(jax.ShapeDtypeStruct((m, D_FF), BF16),) + _tile(tm, tn),
            (jax.ShapeDtypeStruct((m, D_FF), F32),) + _tile(tm, tn),
            (jax.ShapeDtypeStruct((m, D_FF), F32),) + _tile(tm, tn)]
    return _fused_matmul("conv_ffn_up_step", [n2], w_list, extras, outs, epilogue,
                         tm=tm, tn=tn, nj=nj)


def _ple(n3, p, h2, w_ple_gate, w_ple_proj, *, tm, tn):
    m = n3.shape[0]

    def epilogue(accs, e_refs, o_refs, s_refs, j, i):
        gate, proj = accs
        o_refs[0][...] = e_refs[0][...] + jax.nn.sigmoid(gate) * proj

    extras = [(h2,) + _tile(tm, tn)]
    outs = [(jax.ShapeDtypeStruct((m, D_MODEL), F32),) + _tile(tm, tn)]
    return _fused_matmul("ple_gate", [n3, p], [(w_ple_gate, 0, 0), (w_ple_proj, 1, 0)], extras,
                         outs, epilogue, tm=tm, tn=tn, nj=D_MODEL // tn)[0]


def _gla_seq_kernel(q_ref, k_ref, v_ref, r_ref, la_ref, g_ref, xb_ref, sfin_ref,
                    st_ref, b_sc, qt_sc, ks_sc, v_sc, oi_sc, *, seq_len):
    sub, c = GLA_SUB, GLA_CHUNK
    n_chunks = sub // c
    st_ref[...] = jnp.zeros_like(st_ref)
    row = lax.broadcasted_iota(jnp.int32, (sub, sub), 0)
    col = lax.broadcasted_iota(jnp.int32, (sub, sub), 1)
    row_in_chunk = row % c
    causal = ((row // c) == (col // c)) & (col <= row)
    nt = (((1,), (1,)), ((), ()))
    tn = (((0,), (0,)), ((), ()))

    def sub_block(sb, carry):
        r0 = pl.multiple_of(sb * sub, sub)
        b = la_ref[pl.ds(r0, sub), :]
        s = 1
        while s < c:
            b = b + jnp.where(row_in_chunk >= s, pltpu.roll(b, s, axis=0), 0.0)
            s *= 2
        b_sc[...] = b
        q = q_ref[pl.ds(r0, sub), :] * (GLA_DK ** -0.5)
        k = k_ref[pl.ds(r0, sub), :]
        qt = (q * jnp.exp(b)).astype(BF16)
        kt = (k * jnp.exp(-b)).astype(BF16)
        vb = v_ref[pl.ds(r0, sub), :].astype(BF16)
        qt_sc[...] = qt
        v_sc[...] = vb
        decays = []
        for n in range(n_chunks):
            b_last = b_sc[n * c + c - 1:n * c + c, :]
            ks_n = k[n * c:(n + 1) * c, :] * jnp.exp(b_last - b[n * c:(n + 1) * c, :])
            ks_sc[n * c:(n + 1) * c, :] = ks_n.astype(BF16)
            decays.append(jnp.exp(b_last))
        scores = lax.dot_general(qt, kt, nt, preferred_element_type=F32)
        scores = jnp.where(causal, scores, 0.0).astype(BF16)
        o_intra = jnp.dot(scores, vb, preferred_element_type=F32)
        for n in range(n_chunks):
            rows = slice(n * c, (n + 1) * c)
            st = st_ref[...]
            oi_sc[rows, :] = lax.dot_general(qt_sc[rows, :], st.astype(BF16), nt,
                                             preferred_element_type=F32)
            upd = lax.dot_general(v_sc[rows, :], ks_sc[rows, :], tn, preferred_element_type=F32)
            st_ref[...] = st * decays[n] + upd
        o = o_intra + oi_sc[...]
        ms = jnp.mean(o * o, axis=-1, keepdims=True)
        y = (o * lax.rsqrt(ms + EPS)) * g_ref[...]
        r = r_ref[pl.ds(r0, sub), :]
        xb_ref[pl.ds(r0, sub), :] = (y * _silu(r)).astype(BF16)
        return carry

    lax.fori_loop(0, seq_len // sub, sub_block, 0)
    sfin_ref[0, 0] = st_ref[...].T


def _gla_seq(zq, log_a, g_gla, *, n_seq, seq_len):
    h = GLA_HEADS
    blk = (seq_len, GLA_DK)

    def col(off):
        return pl.BlockSpec(blk, functools.partial(lambda b, hh, off: (b, off + hh), off=off))

    est = 2 * 5 * _nbytes(blk, F32) + 2 * _nbytes(blk, BF16) + 16 * GLA_SUB * GLA_SUB * 4
    return pl.pallas_call(
        functools.partial(_gla_seq_kernel, seq_len=seq_len),
        grid=(n_seq, h),
        in_specs=[col(0), col(h), col(2 * h), col(3 * h),
                  pl.BlockSpec(blk, lambda b, hh: (b, hh)),
                  pl.BlockSpec((1, GLA_DV), lambda b, hh: (0, 0))],
        out_specs=[pl.BlockSpec(blk, lambda b, hh: (b, hh)),
                   pl.BlockSpec((1, 1, GLA_DK, GLA_DV), lambda b, hh: (b, hh, 0, 0))],
        out_shape=[jax.ShapeDtypeStruct((n_seq * seq_len, GLA_VAL), BF16),
                   jax.ShapeDtypeStruct((n_seq, h, GLA_DK, GLA_DV), F32)],
        scratch_shapes=[pltpu.VMEM((GLA_DV, GLA_DK), F32),
                        pltpu.VMEM((GLA_SUB, GLA_DK), F32),
                        pltpu.VMEM((GLA_SUB, GLA_DK), BF16),
                        pltpu.VMEM((GLA_SUB, GLA_DK), BF16),
                        pltpu.VMEM((GLA_SUB, GLA_DV), BF16),
                        pltpu.VMEM((GLA_SUB, GLA_DV), F32)],
        compiler_params=pltpu.CompilerParams(
            dimension_semantics=("arbitrary", "arbitrary"),
            vmem_limit_bytes=_vmem_limit(est)),
        name="gla_seq",
    )(zq, zq, zq, zq, log_a, g_gla.reshape(1, GLA_DV))


def _to_column(x_row, eye):
    return jnp.sum(jnp.where(eye, x_row, 0.0), axis=1, keepdims=True)


def _gla_step_kernel(zq_ref, la_ref, g_ref, s0_ref, xb_ref, snew_ref):
    eye = (lax.broadcasted_iota(jnp.int32, (GLA_DK, GLA_DK), 0)
           == lax.broadcasted_iota(jnp.int32, (GLA_DK, GLA_DK), 1))
    pad_rows = 2 * SUBLANES
    for h in range(GLA_HEADS):
        def cols(base, width=GLA_DK, h=h):
            return slice(base + h * width, base + (h + 1) * width)
        q = zq_ref[0, :, cols(0)] * (GLA_DK ** -0.5)
        k = zq_ref[0, :, cols(GLA_KEY)]
        v = zq_ref[0, :, cols(2 * GLA_KEY)]
        r = zq_ref[0, :, cols(2 * GLA_KEY + GLA_VAL)]
        b = la_ref[0, :, cols(0)]
        qt = q * jnp.exp(b)
        kt = k * jnp.exp(-b)
        a = jnp.exp(b)
        s0 = s0_ref[0, h]
        score = jnp.sum(qt * kt, axis=-1, keepdims=True)
        o_inter = jnp.dot(jnp.broadcast_to(qt, (pad_rows, GLA_DK)).astype(BF16), s0.astype(BF16),
                          preferred_element_type=F32)[0:1, :]
        o = score * v + o_inter
        snew_ref[0, h] = s0 * _to_column(a, eye) + _to_column(k, eye) * v
        ms = jnp.mean(o * o, axis=-1, keepdims=True)
        y = (o * lax.rsqrt(ms + EPS)) * g_ref[...]
        xb_ref[0, :, cols(0, GLA_DV)] = (y * _silu(r)).astype(BF16)


def _gla_step(zq, log_a, g_gla, s0):
    n = zq.shape[0]
    wq = zq.shape[1]
    sblk = (1, GLA_HEADS, GLA_DK, GLA_DV)
    est = 4 * _nbytes(sblk, F32) + 8 * GLA_DK * GLA_DV * 4
    xb, s_new = pl.pallas_call(
        _gla_step_kernel,
        grid=(n,),
        in_specs=[pl.BlockSpec((1, 1, wq), lambda b: (b, 0, 0)),
                  pl.BlockSpec((1, 1, GLA_KEY), lambda b: (b, 0, 0)),
                  pl.BlockSpec((1, GLA_DV), lambda b: (0, 0)),
                  pl.BlockSpec(sblk, lambda b: (b, 0, 0, 0))],
        out_specs=[pl.BlockSpec((1, 1, GLA_VAL), lambda b: (b, 0, 0)),
                   pl.BlockSpec(sblk, lambda b: (b, 0, 0, 0))],
        out_shape=[jax.ShapeDtypeStruct((n, 1, GLA_VAL), BF16),
                   jax.ShapeDtypeStruct(s0.shape, F32)],
        compiler_params=pltpu.CompilerParams(
            dimension_semantics=("arbitrary",),
            vmem_limit_bytes=_vmem_limit(est)),
        name="gla_step",
    )(zq.reshape(n, 1, wq), log_a.reshape(n, 1, GLA_KEY), g_gla.reshape(1, GLA_DV), s0)
    return xb.reshape(n, GLA_VAL), s_new


def _tiles(m):
    big = min(m, 1024)
    mid = min(m, 512)
    return big, mid


def _group(x, p, conv_state, gla_state, ffn_state, wts):
    (g_mix, w_in, w_alr, w_gates, w_alpha2, b_alpha, w_conv, w_out_conv, g_gla, w_out_gla,
     w_mix_out, g_ffn, w_up, w_ffn_conv, w_down, g_ple, w_ple_gate, w_ple_proj, g_final) = wts
    n_seq, seq_len, d = x.shape
    m = n_seq * seq_len
    big, mid = _tiles(m)
    h0 = x.reshape(m, d)
    pb = p.reshape(m, p.shape[-1]).astype(BF16)

    n1 = _rmsnorm(h0, g_mix, BF16, mid)
    zq = _qkvr_proj(n1, w_in, tm=big, tn=512)
    log_a = _log_decay(n1, w_alr, w_alpha2, b_alpha, tm=mid)
    if seq_len > 1:
        xa, conv_last = _short_conv_branch(n1, w_in, w_conv, None, seq_len=seq_len, tm=mid, tn=256)
        conv_new = conv_last[:, SUBLANES - (CONV_W - 1):, :]
        xb, gla_new = _gla_seq(zq, log_a, g_gla, n_seq=n_seq, seq_len=seq_len)
    else:
        xa, u = _short_conv_branch(n1, w_in, w_conv, conv_state.reshape(m, -1), seq_len=1,
                                   tm=big, tn=256)
        conv_new = jnp.stack([conv_state[:, 1, :], u], axis=1)
        xb, gla_new = _gla_step(zq, log_a, g_gla, gla_state)

    merged = _merge(n1, xa, xb, w_gates, w_out_conv, w_out_gla, tm=mid, tn=256)
    h1 = _proj_residual("mix_out", merged, w_mix_out, h0, tm=mid, tn=512)

    n2 = _rmsnorm(h1, g_ffn, BF16, mid)
    if seq_len > 1:
        act, last_g, last_v = _conv_ffn_up(n2, w_up, w_ffn_conv, None, seq_len=seq_len,
                                           tm=big, tn=256)
        ffn_new = jnp.concatenate([last_g, last_v], axis=-1)[:, SUBLANES - (CONV_W - 1):, :]
    else:
        act, ug, uv = _conv_ffn_up(n2, w_up, w_ffn_conv, ffn_state.reshape(m, -1), seq_len=1,
                                   tm=big, tn=256)
        ffn_new = jnp.stack([ffn_state[:, 1, :], jnp.concatenate([ug, uv], axis=-1)], axis=1)
    h2 = _proj_residual("ffn_down", act, w_down, h1, tm=mid, tn=256)

    n3 = _rmsnorm(h2, g_ple, BF16, mid)
    h3 = _ple(n3, pb, h2, w_ple_gate, w_ple_proj, tm=mid, tn=512)
    y = _rmsnorm(h3, g_final, F32, mid)
    return y.reshape(n_seq, seq_len, d), conv_new[None], gla_new[None], ffn_new[None]


def kernel(x_prompt, x_sample, p_prompt, p_sample, state_conv, state_gla, state_ffn, g_mix, w_in,
           w_alpha2, b_alpha, w_conv, w_out_conv, g_gla, w_out_gla, w_mix_out, g_ffn, w_up,
           w_ffn_conv, w_down, g_ple, w_ple_gate, w_ple_proj, g_final):
    assert g_mix.shape[0] == 1, "single-layer trunk"
    w_in0 = w_in[0]
    wts = (g_mix[0], w_in0,
           w_in0[:, OFF_ALR:OFF_GATES],
           w_in0[:, OFF_GATES:].astype(BF16),
           w_alpha2[0], b_alpha[0], w_conv[0], w_out_conv[0], g_gla[0], w_out_gla[0],
           w_mix_out[0], g_ffn[0], w_up[0], w_ffn_conv[0], w_down[0].astype(BF16), g_ple[0],
           w_ple_gate[0], w_ple_proj[0], g_final)
    y_p, conv_p, gla_p, ffn_p = _group(x_prompt, p_prompt[0], None, None, None, wts)
    y_s, conv_s, gla_s, ffn_s = _group(x_sample, p_sample[0], state_conv[0], state_gla[0],
                                       state_ffn[0], wts)
    return (y_p, y_s, conv_p, conv_s, gla_p, gla_s, ffn_p, ffn_s)
```

```python
import functools

import jax
import jax.numpy as jnp
from jax import lax
from jax.experimental import pallas as pl
from jax.experimental.pallas import tpu as pltpu

F32 = jnp.float32
BF16 = jnp.bfloat16

D_MODEL = 4096
D_CONV = 2048
CONV_W = 3
GLA_HEADS = 8
GLA_DK = 256
GLA_DV = 256
GLA_KEY = GLA_HEADS * GLA_DK
GLA_VAL = GLA_HEADS * GLA_DV
GATE_RANK = 16
GATE_NORM = 16.0
GLA_CHUNK = 32
D_FF = 11008
EPS = 1e-6

V7X_VMEM_BYTES = 64 * 2**20
V7X_VMEM_CAP = 60 * 2**20
SUBLANES = 8
LANES = 128
MXU_DIM = 256
GLA_SUB = MXU_DIM
CAST_CHUNK_ELEMS = 256 * 1024

OFF_HA = 0
OFF_CA = D_CONV
OFF_BA = 2 * D_CONV
OFF_QKVR = 3 * D_CONV
OFF_ALR = OFF_QKVR + 2 * GLA_KEY + 2 * GLA_VAL
OFF_GATE_A = OFF_ALR + GATE_RANK
OFF_GATE_B = OFF_GATE_A + D_MODEL

_NT = (((1,), (1,)), ((), ()))
_TN = (((0,), (0,)), ((), ()))


def _vmem_limit(nbytes):
    return int(min(V7X_VMEM_CAP, nbytes + 8 * 2**20))


def _nbytes(shape, dtype):
    n = 1
    for s in shape:
        n *= int(s)
    return n * jnp.dtype(dtype).itemsize


def _largest_divisor(n, limit, multiple):
    best = multiple
    for d in range(multiple, min(n, limit) + 1, multiple):
        if n % d == 0:
            best = d
    assert n % best == 0
    return best


def _rmsnorm_kernel(x_ref, g_ref, o_ref):
    x = x_ref[...]
    ms = jnp.mean(x * x, axis=-1, keepdims=True)
    o_ref[...] = ((x * lax.rsqrt(ms + EPS)) * g_ref[...]).astype(o_ref.dtype)


def _rmsnorm(x, g, out_dtype, tm):
    m, d = x.shape
    return pl.pallas_call(
        _rmsnorm_kernel,
        grid=(m // tm,),
        in_specs=[pl.BlockSpec((tm, d), lambda i: (i, 0)),
                  pl.BlockSpec((1, d), lambda i: (0, 0))],
        out_specs=pl.BlockSpec((tm, d), lambda i: (i, 0)),
        out_shape=jax.ShapeDtypeStruct((m, d), out_dtype),
        compiler_params=pltpu.CompilerParams(
            dimension_semantics=("arbitrary",),
            vmem_limit_bytes=_vmem_limit(6 * tm * d * 4)),
        name="rmsnorm",
    )(x, g.reshape(1, d))


def _cast_rows(src_ref, dst_ref):
    rows, cols = src_ref.shape
    ck = _largest_divisor(rows, max(2 * SUBLANES, CAST_CHUNK_ELEMS // cols), 2 * SUBLANES)

    def body(c, carry):
        r = pl.multiple_of(c * ck, ck)
        dst_ref[pl.ds(r, ck), :] = src_ref[pl.ds(r, ck), :].astype(BF16)
        return carry

    lax.fori_loop(0, rows // ck, body, 0)


def _fused_matmul(name, a_list, w_list, extra_list, out_list, epilogue, *, tm, tn, nj,
                  scratch=(), acc_history=False, zero_scratch=False):
    m = a_list[0].shape[0]
    assert m % tm == 0
    ni = m // tm
    na, nw, ne, no = len(a_list), len(w_list), len(extra_list), len(out_list)
    cast_ids = [n for n, w in enumerate(w_list) if w[0].dtype != BF16]

    def w_block(w, off, transposed):
        if transposed:
            assert off % SUBLANES == 0 and tn % SUBLANES == 0
            return ((pl.Element(tn), pl.Element(w.shape[1])),
                    (lambda j, i: (pl.multiple_of(off + j * tn, SUBLANES), 0)))
        assert off % tn == 0
        return (w.shape[0], tn), (lambda j, i: (0, off // tn + j))

    def w_tile_shape(w, transposed):
        return (tn, w.shape[1]) if transposed else (w.shape[0], tn)

    in_specs = [pl.BlockSpec((tm, a.shape[1]), lambda j, i: (i, 0)) for a in a_list]
    in_specs += [pl.BlockSpec(*w_block(w, off, tr)) for w, _, off, tr in w_list]
    in_specs += [pl.BlockSpec(blk, im) for _, blk, im in extra_list]
    out_specs = [pl.BlockSpec(blk, im) for _, blk, im in out_list]
    out_shape = [s for s, _, _ in out_list]
    scratch_shapes = [pltpu.VMEM(w_tile_shape(w_list[n][0], w_list[n][3]), BF16) for n in cast_ids]
    n_acc = nw if acc_history else 0
    scratch_shapes += [pltpu.VMEM((SUBLANES + tm, tn), F32)] * n_acc
    scratch_shapes += list(scratch)

    def kernel(*refs):
        a_refs = refs[:na]
        w_refs = refs[na:na + nw]
        e_refs = refs[na + nw:na + nw + ne]
        o_refs = refs[na + nw + ne:na + nw + ne + no]
        s_refs = refs[na + nw + ne + no:]
        wb_refs = dict(zip(cast_ids, s_refs[:len(cast_ids)]))
        acc_refs = s_refs[len(cast_ids):len(cast_ids) + n_acc]
        user_scratch = s_refs[len(cast_ids) + n_acc:]
        j = pl.program_id(0)
        i = pl.program_id(1)

        @pl.when(i == 0)
        def _():
            for n in cast_ids:
                _cast_rows(w_refs[n], wb_refs[n])
            for ref in list(acc_refs) + (list(user_scratch) if zero_scratch else []):
                ref[...] = jnp.zeros_like(ref)

        accs = []
        for n, (_, ai, _, transposed) in enumerate(w_list):
            w_ref = wb_refs[n] if n in wb_refs else w_refs[n]
            a = a_refs[ai][...]
            if transposed:
                acc = lax.dot_general(a, w_ref[...], _NT, preferred_element_type=F32)
            else:
                acc = jnp.dot(a, w_ref[...], preferred_element_type=F32)
            if acc_history:
                _push_history(acc_refs[n], acc)
                accs.append(acc_refs[n])
            else:
                accs.append(acc)
        epilogue(accs, e_refs, o_refs, user_scratch, j, i)

    est = 0
    for a in a_list:
        est += 2 * _nbytes((tm, a.shape[1]), a.dtype)
    for w, _, _, tr in w_list:
        est += 2 * _nbytes(w_tile_shape(w, tr), w.dtype)
        if w.dtype != BF16:
            est += _nbytes(w_tile_shape(w, tr), BF16)
    for arr, blk, _ in list(extra_list) + list(out_list):
        est += 2 * _nbytes(blk, arr.dtype)
    est += 3 * nw * tm * tn * 4

    return pl.pallas_call(
        kernel,
        grid=(nj, ni),
        in_specs=in_specs,
        out_specs=out_specs,
        out_shape=out_shape,
        scratch_shapes=scratch_shapes,
        compiler_params=pltpu.CompilerParams(
            dimension_semantics=("arbitrary", "arbitrary"),
            vmem_limit_bytes=_vmem_limit(est)),
        name=name,
    )(*a_list, *[w[0] for w in w_list], *[e for e, _, _ in extra_list])


def _tile(tm, tn):
    return (tm, tn), (lambda j, i: (i, j))


def _push_history(hist_ref, tile):
    tm = tile.shape[0]
    prev_tail = hist_ref[tm:tm + SUBLANES, :]
    hist_ref[SUBLANES:SUBLANES + tm, :] = tile
    hist_ref[0:SUBLANES, :] = prev_tail


def _conv3_history(hist_ref, seq_start, w3):
    tm = hist_ref.shape[0] - SUBLANES
    hist_ref[0:SUBLANES, :] = jnp.where(seq_start, 0.0, hist_ref[0:SUBLANES, :])
    taps = [hist_ref[SUBLANES - (CONV_W - 1) + t:SUBLANES - (CONV_W - 1) + t + tm, :]
            for t in range(CONV_W)]
    return w3[0:1, :] * taps[0] + w3[1:2, :] * taps[1] + w3[2:3, :] * taps[2]


def _history_tail(hist_ref):
    tm = hist_ref.shape[0] - SUBLANES
    return hist_ref[tm:tm + SUBLANES, :]


def _silu(x):
    return x * jax.nn.sigmoid(x)


def _short_conv_branch(n1, w_in_t, w_conv, state2d, *, seq_len, tm, tn):
    m = n1.shape[0]
    nj = D_CONV // tn
    w_list = [(w_in_t, 0, OFF_HA, True), (w_in_t, 0, OFF_CA, True), (w_in_t, 0, OFF_BA, True)]
    wc = (w_conv, (CONV_W, tn), lambda j, i: (0, j))
    if seq_len > 1:
        tps = seq_len // tm
        n_seq = m // seq_len

        def epilogue(accs, e_refs, o_refs, s_refs, j, i):
            h, c, b = accs
            (wc_ref,), (xa_ref, last_ref), (u_hist,) = e_refs, o_refs, s_refs
            _push_history(u_hist, c * h)
            cu = _conv3_history(u_hist, i % tps == 0, wc_ref[...])
            xa_ref[...] = (b * cu).astype(BF16)
            last_ref[0] = _history_tail(u_hist)

        outs = [(jax.ShapeDtypeStruct((m, D_CONV), BF16),) + _tile(tm, tn),
                (jax.ShapeDtypeStruct((n_seq, SUBLANES, D_CONV), F32), (1, SUBLANES, tn),
                 lambda j, i: (i // tps, 0, j))]
        return _fused_matmul("short_conv_seq", [n1], w_list, [wc], outs, epilogue,
                             tm=tm, tn=tn, nj=nj,
                             scratch=[pltpu.VMEM((SUBLANES + tm, tn), F32)], zero_scratch=True)

    def epilogue(accs, e_refs, o_refs, s_refs, j, i):
        h, c, b = accs
        wc_ref, s0_ref, s1_ref = e_refs
        xa_ref, u_ref = o_refs
        u = c * h
        w3 = wc_ref[...]
        cu = w3[0:1, :] * s0_ref[...] + w3[1:2, :] * s1_ref[...] + w3[2:3, :] * u
        xa_ref[...] = (b * cu).astype(BF16)
        u_ref[...] = u

    extras = [wc,
              (state2d, (tm, tn), lambda j, i: (i, j)),
              (state2d, (tm, tn), lambda j, i: (i, D_CONV // tn + j))]
    outs = [(jax.ShapeDtypeStruct((m, D_CONV), BF16),) + _tile(tm, tn),
            (jax.ShapeDtypeStruct((m, D_CONV), F32),) + _tile(tm, tn)]
    return _fused_matmul("short_conv_step", [n1], w_list, extras, outs, epilogue,
                         tm=tm, tn=tn, nj=nj)


def _qkvr_proj(n1, w_in_t, *, tm, tn):
    m = n1.shape[0]
    n = 2 * GLA_KEY + 2 * GLA_VAL

    def epilogue(accs, e_refs, o_refs, s_refs, j, i):
        o_refs[0][...] = accs[0]

    outs = [(jax.ShapeDtypeStruct((m, n), F32),) + _tile(tm, tn)]
    return _fused_matmul("qkvr_proj", [n1], [(w_in_t, 0, OFF_QKVR, True)], [], outs, epilogue,
                         tm=tm, tn=tn, nj=n // tn)[0]


def _log_decay_kernel(n_ref, w1_ref, w2_ref, b_ref, o_ref):
    a_lr = lax.dot_general(n_ref[...], w1_ref[...], _NT, preferred_element_type=F32)
    x = jnp.dot(a_lr.astype(BF16), w2_ref[...], preferred_element_type=F32) + b_ref[...]
    log_sig = jnp.minimum(x, 0.0) - jnp.log1p(jnp.exp(-jnp.abs(x)))
    o_ref[...] = log_sig / GATE_NORM


def _log_decay(n1, w_alr_t, w_alpha2, b_alpha, *, tm):
    m, d = n1.shape
    w1 = jnp.pad(w_alr_t, ((0, LANES - GATE_RANK), (0, 0))).astype(BF16)
    w2 = jnp.pad(w_alpha2, ((0, LANES - GATE_RANK), (0, 0))).astype(BF16)
    return pl.pallas_call(
        _log_decay_kernel,
        grid=(m // tm,),
        in_specs=[pl.BlockSpec((tm, d), lambda i: (i, 0)),
                  pl.BlockSpec((LANES, d), lambda i: (0, 0)),
                  pl.BlockSpec((LANES, GLA_KEY), lambda i: (0, 0)),
                  pl.BlockSpec((1, GLA_KEY), lambda i: (0, 0))],
        out_specs=pl.BlockSpec((tm, GLA_KEY), lambda i: (i, 0)),
        out_shape=jax.ShapeDtypeStruct((m, GLA_KEY), F32),
        compiler_params=pltpu.CompilerParams(
            dimension_semantics=("arbitrary",),
            vmem_limit_bytes=_vmem_limit(2 * tm * d * 2 + 6 * tm * GLA_KEY * 4 + 4 * 2**20)),
        name="log_decay",
    )(n1, w1, w2, b_alpha.reshape(1, GLA_KEY))


def _merge(n1, xa, xb, w_in_t, w_out_conv, w_out_gla, *, tm, tn):
    m = n1.shape[0]

    def epilogue(accs, e_refs, o_refs, s_refs, j, i):
        ga, gb, ya, yb = accs
        o_refs[0][...] = (jax.nn.sigmoid(ga) * ya + jax.nn.sigmoid(gb) * yb).astype(BF16)

    w_list = [(w_in_t, 0, OFF_GATE_A, True), (w_in_t, 0, OFF_GATE_B, True),
              (w_out_conv, 1, 0, False), (w_out_gla, 2, 0, False)]
    outs = [(jax.ShapeDtypeStruct((m, D_MODEL), BF16),) + _tile(tm, tn)]
    return _fused_matmul("gated_merge", [n1, xa, xb], w_list, [], outs, epilogue,
                         tm=tm, tn=tn, nj=D_MODEL // tn)[0]


def _proj_residual(name, a, w, res, *, tm, tn):
    m = a.shape[0]
    n = w.shape[1]

    def epilogue(accs, e_refs, o_refs, s_refs, j, i):
        o_refs[0][...] = e_refs[0][...] + accs[0]

    extras = [(res,) + _tile(tm, tn)]
    outs = [(jax.ShapeDtypeStruct((m, n), F32),) + _tile(tm, tn)]
    return _fused_matmul(name, [a], [(w, 0, 0, False)], extras, outs, epilogue,
                         tm=tm, tn=tn, nj=n // tn)[0]


def _conv_ffn_up(n2, w_up, w_ffn_conv, state2d, *, seq_len, tm, tn):
    m = n2.shape[0]
    nj = D_FF // tn
    w_list = [(w_up, 0, 0, False), (w_up, 0, D_FF, False)]
    wcg = (w_ffn_conv, (CONV_W, tn), lambda j, i: (0, j))
    wcv = (w_ffn_conv, (CONV_W, tn), lambda j, i: (0, nj + j))
    if seq_len > 1:
        tps = seq_len // tm
        n_seq = m // seq_len

        def epilogue(accs, e_refs, o_refs, s_refs, j, i):
            ug_hist, uv_hist = accs
            wg_ref, wv_ref = e_refs
            act_ref, lastg_ref, lastv_ref = o_refs
            seq_start = i % tps == 0
            cg = _conv3_history(ug_hist, seq_start, wg_ref[...])
            cv = _conv3_history(uv_hist, seq_start, wv_ref[...])
            act_ref[...] = (_silu(cg) * cv).astype(BF16)
            lastg_ref[0] = _history_tail(ug_hist)
            lastv_ref[0] = _history_tail(uv_hist)

        last = (jax.ShapeDtypeStruct((n_seq, SUBLANES, D_FF), F32), (1, SUBLANES, tn),
                lambda j, i: (i // tps, 0, j))
        outs = [(jax.ShapeDtypeStruct((m, D_FF), BF16),) + _tile(tm, tn), last, last]
        return _fused_matmul("conv_ffn_up_seq", [n2], w_list, [wcg, wcv], outs, epilogue,
                             tm=tm, tn=tn, nj=nj, acc_history=True)

    def epilogue(accs, e_refs, o_refs, s_refs, j, i):
        ug, uv = accs
        wg_ref, wv_ref, g0, v0, g1, v1 = e_refs
        act_ref, ug_ref, uv_ref = o_refs
        wg = wg_ref[...]
        wv = wv_ref[...]
        cg = wg[0:1, :] * g0[...] + wg[1:2, :] * g1[...] + wg[2:3, :] * ug
        cv = wv[0:1, :] * v0[...] + wv[1:2, :] * v1[...] + wv[2:3, :] * uv
        act_ref[...] = (_silu(cg) * cv).astype(BF16)
        ug_ref[...] = ug
        uv_ref[...] = uv

    def st(off):
        return (state2d, (tm, tn), functools.partial(lambda j, i, off: (i, off + j), off=off))

    extras = [wcg, wcv, st(0), st(nj), st(2 * nj), st(3 * nj)]
    outs = [(jax.ShapeDtypeStruct((m, D_FF), BF16),) + _tile(tm, tn),
            (jax.ShapeDtypeStruct((m, D_FF), F32),) + _tile(tm, tn),
            (jax.ShapeDtypeStruct((m, D_FF), F32),) + _tile(tm, tn)]
    return _fused_matmul("conv_ffn_up_step", [n2], w_list, extras, outs, epilogue,
                         tm=tm, tn=tn, nj=nj)


def _ple(n3, p, h2, w_ple_gate, w_ple_proj, *, tm, tn):
    m = n3.shape[0]

    def epilogue(accs, e_refs, o_refs, s_refs, j, i):
        gate, proj = accs
        o_refs[0][...] = e_refs[0][...] + jax.nn.sigmoid(gate) * proj

    extras = [(h2,) + _tile(tm, tn)]
    outs = [(jax.ShapeDtypeStruct((m, D_MODEL), F32),) + _tile(tm, tn)]
    return _fused_matmul("ple_gate", [n3, p],
                         [(w_ple_gate, 0, 0, False), (w_ple_proj, 1, 0, False)], extras,
                         outs, epilogue, tm=tm, tn=tn, nj=D_MODEL // tn)[0]


def _gla_seq_kernel(q_ref, k_ref, v_ref, r_ref, la_ref, g_ref, xb_ref, sfin_ref,
                    st_ref, b_sc, qt_sc, ks_sc, v_sc, oi_sc, *, seq_len):
    sub, c = GLA_SUB, GLA_CHUNK
    n_chunks = sub // c
    st_ref[...] = jnp.zeros_like(st_ref)
    row = lax.broadcasted_iota(jnp.int32, (sub, sub), 0)
    col = lax.broadcasted_iota(jnp.int32, (sub, sub), 1)
    row_in_chunk = row % c
    causal = ((row // c) == (col // c)) & (col <= row)

    def sub_block(sb, carry):
        r0 = pl.multiple_of(sb * sub, sub)
        b = la_ref[pl.ds(r0, sub), :]
        s = 1
        while s < c:
            b = b + jnp.where(row_in_chunk >= s, pltpu.roll(b, s, axis=0), 0.0)
            s *= 2
        b_sc[...] = b
        q = q_ref[pl.ds(r0, sub), :] * (GLA_DK ** -0.5)
        k = k_ref[pl.ds(r0, sub), :]
        qt = (q * jnp.exp(b)).astype(BF16)
        kt = (k * jnp.exp(-b)).astype(BF16)
        vb = v_ref[pl.ds(r0, sub), :].astype(BF16)
        qt_sc[...] = qt
        v_sc[...] = vb
        decays = []
        for n in range(n_chunks):
            b_last = b_sc[n * c + c - 1:n * c + c, :]
            ks_n = k[n * c:(n + 1) * c, :] * jnp.exp(b_last - b[n * c:(n + 1) * c, :])
            ks_sc[n * c:(n + 1) * c, :] = ks_n.astype(BF16)
            decays.append(jnp.exp(b_last))
        scores = lax.dot_general(qt, kt, _NT, preferred_element_type=F32)
        scores = jnp.where(causal, scores, 0.0).astype(BF16)
        o_intra = jnp.dot(scores, vb, preferred_element_type=F32)
        for n in range(n_chunks):
            rows = slice(n * c, (n + 1) * c)
            st = st_ref[...]
            oi_sc[rows, :] = lax.dot_general(qt_sc[rows, :], st.astype(BF16), _NT,
                                             preferred_element_type=F32)
            upd = lax.dot_general(v_sc[rows, :], ks_sc[rows, :], _TN, preferred_element_type=F32)
            st_ref[...] = st * decays[n] + upd
        o = o_intra + oi_sc[...]
        ms = jnp.mean(o * o, axis=-1, keepdims=True)
        y = (o * lax.rsqrt(ms + EPS)) * g_ref[...]
        r = r_ref[pl.ds(r0, sub), :]
        xb_ref[pl.ds(r0, sub), :] = (y * _silu(r)).astype(BF16)
        return carry

    lax.fori_loop(0, seq_len // sub, sub_block, 0)
    sfin_ref[0, 0] = st_ref[...].T


def _gla_seq(zq, log_a, g_gla, *, n_seq, seq_len):
    h = GLA_HEADS
    blk = (seq_len, GLA_DK)

    def col(off):
        return pl.BlockSpec(blk, functools.partial(lambda b, hh, off: (b, off + hh), off=off))

    est = 2 * 5 * _nbytes(blk, F32) + 2 * _nbytes(blk, BF16) + 16 * GLA_SUB * GLA_SUB * 4
    return pl.pallas_call(
        functools.partial(_gla_seq_kernel, seq_len=seq_len),
        grid=(n_seq, h),
        in_specs=[col(0), col(h), col(2 * h), col(3 * h),
                  pl.BlockSpec(blk, lambda b, hh: (b, hh)),
                  pl.BlockSpec((1, GLA_DV), lambda b, hh: (0, 0))],
        out_specs=[pl.BlockSpec(blk, lambda b, hh: (b, hh)),
                   pl.BlockSpec((1, 1, GLA_DK, GLA_DV), lambda b, hh: (b, hh, 0, 0))],
        out_shape=[jax.ShapeDtypeStruct((n_seq * seq_len, GLA_VAL), BF16),
                   jax.ShapeDtypeStruct((n_seq, h, GLA_DK, GLA_DV), F32)],
        scratch_shapes=[pltpu.VMEM((GLA_DV, GLA_DK), F32),
                        pltpu.VMEM((GLA_SUB, GLA_DK), F32),
                        pltpu.VMEM((GLA_SUB, GLA_DK), BF16),
                        pltpu.VMEM((GLA_SUB, GLA_DK), BF16),
                        pltpu.VMEM((GLA_SUB, GLA_DV), BF16),
                        pltpu.VMEM((GLA_SUB, GLA_DV), F32)],
        compiler_params=pltpu.CompilerParams(
            dimension_semantics=("arbitrary", "arbitrary"),
            vmem_limit_bytes=_vmem_limit(est)),
        name="gla_seq",
    )(zq, zq, zq, zq, log_a, g_gla.reshape(1, GLA_DV))


def _to_column(x_row, eye):
    return jnp.sum(jnp.where(eye, x_row, 0.0), axis=1, keepdims=True)


def _gla_step_kernel(zq_ref, la_ref, g_ref, s0_ref, xb_ref, snew_ref):
    eye = (lax.broadcasted_iota(jnp.int32, (GLA_DK, GLA_DK), 0)
           == lax.broadcasted_iota(jnp.int32, (GLA_DK, GLA_DK), 1))
    pad_rows = 2 * SUBLANES
    for h in range(GLA_HEADS):
        def cols(base, width=GLA_DK, h=h):
            return slice(base + h * width, base + (h + 1) * width)
        q = zq_ref[0, :, cols(0)] * (GLA_DK ** -0.5)
        k = zq_ref[0, :, cols(GLA_KEY)]
        v = zq_ref[0, :, cols(2 * GLA_KEY)]
        r = zq_ref[0, :, cols(2 * GLA_KEY + GLA_VAL)]
        b = la_ref[0, :, cols(0)]
        qt = q * jnp.exp(b)
        kt = k * jnp.exp(-b)
        a = jnp.exp(b)
        s0 = s0_ref[0, h]
        score = jnp.sum(qt * kt, axis=-1, keepdims=True)
        o_inter = jnp.dot(jnp.broadcast_to(qt, (pad_rows, GLA_DK)).astype(BF16), s0.astype(BF16),
                          preferred_element_type=F32)[0:1, :]
        o = score * v + o_inter
        snew_ref[0, h] = s0 * _to_column(a, eye) + _to_column(k, eye) * v
        ms = jnp.mean(o * o, axis=-1, keepdims=True)
        y = (o * lax.rsqrt(ms + EPS)) * g_ref[...]
        xb_ref[0, :, cols(0, GLA_DV)] = (y * _silu(r)).astype(BF16)


def _gla_step(zq, log_a, g_gla, s0):
    n = zq.shape[0]
    wq = zq.shape[1]
    sblk = (1, GLA_HEADS, GLA_DK, GLA_DV)
    est = 4 * _nbytes(sblk, F32) + 8 * GLA_DK * GLA_DV * 4
    xb, s_new = pl.pallas_call(
        _gla_step_kernel,
        grid=(n,),
        in_specs=[pl.BlockSpec((1, 1, wq), lambda b: (b, 0, 0)),
                  pl.BlockSpec((1, 1, GLA_KEY), lambda b: (b, 0, 0)),
                  pl.BlockSpec((1, GLA_DV), lambda b: (0, 0)),
                  pl.BlockSpec(sblk, lambda b: (b, 0, 0, 0))],
        out_specs=[pl.BlockSpec((1, 1, GLA_VAL), lambda b: (b, 0, 0)),
                   pl.BlockSpec(sblk, lambda b: (b, 0, 0, 0))],
        out_shape=[jax.ShapeDtypeStruct((n, 1, GLA_VAL), BF16),
                   jax.ShapeDtypeStruct(s0.shape, F32)],
        compiler_params=pltpu.CompilerParams(
            dimension_semantics=("arbitrary",),
            vmem_limit_bytes=_vmem_limit(est)),
        name="gla_step",
    )(zq.reshape(n, 1, wq), log_a.reshape(n, 1, GLA_KEY), g_gla.reshape(1, GLA_DV), s0)
    return xb.reshape(n, GLA_VAL), s_new


def _plan(m):
    big, mid = min(m, 1024), min(m, 512)
    return dict(
        norm=mid,
        short_conv=dict(tm=mid, tn=256),
        qkvr=dict(tm=big, tn=512),
        log_decay=mid,
        merge=dict(tm=mid, tn=256),
        mix_out=dict(tm=big, tn=512),
        ffn_up=dict(tm=big, tn=256),
        ffn_down=dict(tm=mid, tn=512),
        ple=dict(tm=big, tn=512),
    )


def _group(x, p, conv_state, gla_state, ffn_state, wts):
    (g_mix, w_in_t, w_alpha2, b_alpha, w_conv, w_out_conv, g_gla, w_out_gla,
     w_mix_out, g_ffn, w_up, w_ffn_conv, w_down, g_ple, w_ple_gate, w_ple_proj, g_final) = wts
    n_seq, seq_len, d = x.shape
    m = n_seq * seq_len
    plan = _plan(m)
    h0 = x.reshape(m, d)
    pb = p.reshape(m, p.shape[-1]).astype(BF16)

    n1 = _rmsnorm(h0, g_mix, BF16, plan["norm"])
    zq = _qkvr_proj(n1, w_in_t, **plan["qkvr"])
    log_a = _log_decay(n1, w_in_t[OFF_ALR:OFF_GATE_A], w_alpha2, b_alpha, tm=plan["log_decay"])
    if seq_len > 1:
        xa, conv_last = _short_conv_branch(n1, w_in_t, w_conv, None, seq_len=seq_len,
                                           **plan["short_conv"])
        conv_new = conv_last[:, SUBLANES - (CONV_W - 1):, :]
        xb, gla_new = _gla_seq(zq, log_a, g_gla, n_seq=n_seq, seq_len=seq_len)
    else:
        xa, u = _short_conv_branch(n1, w_in_t, w_conv, conv_state.reshape(m, -1), seq_len=1,
                                   **plan["short_conv"])
        conv_new = jnp.stack([conv_state[:, 1, :], u], axis=1)
        xb, gla_new = _gla_step(zq, log_a, g_gla, gla_state)

    merged = _merge(n1, xa, xb, w_in_t, w_out_conv, w_out_gla, **plan["merge"])
    h1 = _proj_residual("mix_out", merged, w_mix_out, h0, **plan["mix_out"])

    n2 = _rmsnorm(h1, g_ffn, BF16, plan["norm"])
    if seq_len > 1:
        act, last_g, last_v = _conv_ffn_up(n2, w_up, w_ffn_conv, None, seq_len=seq_len,
                                           **plan["ffn_up"])
        ffn_new = jnp.concatenate([last_g, last_v], axis=-1)[:, SUBLANES - (CONV_W - 1):, :]
    else:
        act, ug, uv = _conv_ffn_up(n2, w_up, w_ffn_conv, ffn_state.reshape(m, -1), seq_len=1,
                                   **plan["ffn_up"])
        ffn_new = jnp.stack([ffn_state[:, 1, :], jnp.concatenate([ug, uv], axis=-1)], axis=1)
    h2 = _proj_residual("ffn_down", act, w_down, h1, **plan["ffn_down"])

    n3 = _rmsnorm(h2, g_ple, BF16, plan["norm"])
    h3 = _ple(n3, pb, h2, w_ple_gate, w_ple_proj, **plan["ple"])
    y = _rmsnorm(h3, g_final, F32, plan["norm"])
    return y.reshape(n_seq, seq_len, d), conv_new[None], gla_new[None], ffn_new[None]


def kernel(x_prompt, x_sample, p_prompt, p_sample, state_conv, state_gla, state_ffn, g_mix, w_in,
           w_alpha2, b_alpha, w_conv, w_out_conv, g_gla, w_out_gla, w_mix_out, g_ffn, w_up,
           w_ffn_conv, w_down, g_ple, w_ple_gate, w_ple_proj, g_final):
    assert g_mix.shape[0] == 1, "single-layer trunk"
    wts = (g_mix[0], jnp.swapaxes(w_in[0], 0, 1),
           w_alpha2[0], b_alpha[0], w_conv[0], w_out_conv[0], g_gla[0], w_out_gla[0],
           w_mix_out[0], g_ffn[0], w_up[0], w_ffn_conv[0], w_down[0].astype(BF16), g_ple[0],
           w_ple_gate[0], w_ple_proj[0], g_final)
    y_p, conv_p, gla_p, ffn_p = _group(x_prompt, p_prompt[0], None, None, None, wts)
    y_s, conv_s, gla_s, ffn_s = _group(x_sample, p_sample[0], state_conv[0], state_gla[0],
                                       state_ffn[0], wts)
    return (y_p, y_s, conv_p, conv_s, gla_p, gla_s, ffn_p, ffn_s)
```

```python
import functools

import jax
import jax.numpy as jnp
from jax import lax
from jax.experimental import pallas as pl
from jax.experimental.pallas import tpu as pltpu

F32 = jnp.float32
BF16 = jnp.bfloat16

D_MODEL = 4096
D_CONV = 2048
CONV_W = 3
GLA_HEADS = 8
GLA_DK = 256
GLA_DV = 256
GLA_KEY = GLA_HEADS * GLA_DK
GLA_VAL = GLA_HEADS * GLA_DV
GATE_RANK = 16
GATE_NORM = 16.0
GLA_CHUNK = 32
D_FF = 11008
EPS = 1e-6

V7X_VMEM_BYTES = 64 * 2**20
V7X_VMEM_CAP = 60 * 2**20
SUBLANES = 8
LANES = 128
MXU_DIM = 256
GLA_SUB = MXU_DIM
CAST_CHUNK_ELEMS = 256 * 1024

OFF_HA = 0
OFF_CA = D_CONV
OFF_BA = 2 * D_CONV
OFF_QKVR = 3 * D_CONV
OFF_ALR = OFF_QKVR + 2 * GLA_KEY + 2 * GLA_VAL
OFF_GATE_A = OFF_ALR + GATE_RANK
OFF_GATE_B = OFF_GATE_A + D_MODEL

_NT = (((1,), (1,)), ((), ()))
_TN = (((0,), (0,)), ((), ()))


def _vmem_limit(nbytes):
    return int(min(V7X_VMEM_CAP, nbytes + 8 * 2**20))


def _nbytes(shape, dtype):
    n = 1
    for s in shape:
        n *= int(s)
    return n * jnp.dtype(dtype).itemsize


def _largest_divisor(n, limit, multiple):
    best = multiple
    for d in range(multiple, min(n, limit) + 1, multiple):
        if n % d == 0:
            best = d
    assert n % best == 0
    return best


def _rmsnorm_kernel(x_ref, g_ref, o_ref):
    x = x_ref[...]
    ms = jnp.mean(x * x, axis=-1, keepdims=True)
    o_ref[...] = ((x * lax.rsqrt(ms + EPS)) * g_ref[...]).astype(o_ref.dtype)


def _rmsnorm(x, g, out_dtype, tm):
    m, d = x.shape
    tm = min(m, tm)
    return pl.pallas_call(
        _rmsnorm_kernel,
        grid=(m // tm,),
        in_specs=[pl.BlockSpec((tm, d), lambda i: (i, 0)),
                  pl.BlockSpec((1, d), lambda i: (0, 0))],
        out_specs=pl.BlockSpec((tm, d), lambda i: (i, 0)),
        out_shape=jax.ShapeDtypeStruct((m, d), out_dtype),
        compiler_params=pltpu.CompilerParams(
            dimension_semantics=("arbitrary",),
            vmem_limit_bytes=_vmem_limit(6 * tm * d * 4)),
        name="rmsnorm",
    )(x, g.reshape(1, d))


def _cast_rows(src_ref, dst_ref):
    rows, cols = src_ref.shape
    ck = _largest_divisor(rows, max(2 * SUBLANES, CAST_CHUNK_ELEMS // cols), 2 * SUBLANES)

    def body(c, carry):
        r = pl.multiple_of(c * ck, ck)
        dst_ref[pl.ds(r, ck), :] = src_ref[pl.ds(r, ck), :].astype(BF16)
        return carry

    lax.fori_loop(0, rows // ck, body, 0)


def _fused_matmul(name, a_list, w_list, extra_list, out_list, epilogue, *, tm, tn, nj,
                  scratch=(), acc_history=False, zero_scratch=False, tail=None):
    m = a_list[0].shape[0]
    assert m % tm == 0
    ni = m // tm
    t_a, t_extra, t_out, t_epilogue = tail if tail is not None else ([], [], [], None)
    assert len(t_a) in (0, len(a_list))
    na, nw, ne, no = len(a_list), len(w_list), len(extra_list), len(out_list)
    nta, nte, nto = len(t_a), len(t_extra), len(t_out)
    cast_ids = [n for n, w in enumerate(w_list) if w[0].dtype != BF16]

    def w_block(w, off, transposed):
        if transposed:
            assert off % SUBLANES == 0 and tn % SUBLANES == 0
            return ((pl.Element(tn), pl.Element(w.shape[1])),
                    (lambda j, i: (pl.multiple_of(off + j * tn, SUBLANES), 0)))
        assert off % tn == 0
        return (w.shape[0], tn), (lambda j, i: (0, off // tn + j))

    def w_tile_shape(w, transposed):
        return (tn, w.shape[1]) if transposed else (w.shape[0], tn)

    in_specs = [pl.BlockSpec((tm, a.shape[1]), lambda j, i: (i, 0)) for a in a_list]
    in_specs += [pl.BlockSpec(a.shape, lambda j, i: (0, 0)) for a in t_a]
    in_specs += [pl.BlockSpec(*w_block(w, off, tr)) for w, _, off, tr in w_list]
    in_specs += [pl.BlockSpec(blk, im) for _, blk, im in list(extra_list) + list(t_extra)]
    out_specs = [pl.BlockSpec(blk, im) for _, blk, im in list(out_list) + list(t_out)]
    out_shape = [s for s, _, _ in list(out_list) + list(t_out)]
    scratch_shapes = [pltpu.VMEM(w_tile_shape(w_list[n][0], w_list[n][3]), BF16) for n in cast_ids]
    n_acc = nw if acc_history else 0
    scratch_shapes += [pltpu.VMEM((SUBLANES + tm, tn), F32)] * n_acc
    scratch_shapes += list(scratch)

    def kernel(*refs):
        pos = 0

        def take(count):
            nonlocal pos
            pos += count
            return refs[pos - count:pos]

        a_refs, ta_refs, w_refs = take(na), take(nta), take(nw)
        e_refs, te_refs, o_refs, to_refs = take(ne), take(nte), take(no), take(nto)
        s_refs = refs[pos:]
        wb_refs = dict(zip(cast_ids, s_refs[:len(cast_ids)]))
        acc_refs = s_refs[len(cast_ids):len(cast_ids) + n_acc]
        user_scratch = s_refs[len(cast_ids) + n_acc:]
        j = pl.program_id(0)
        i = pl.program_id(1)

        @pl.when(i == 0)
        def _():
            for n in cast_ids:
                _cast_rows(w_refs[n], wb_refs[n])
            for ref in list(acc_refs) + (list(user_scratch) if zero_scratch else []):
                ref[...] = jnp.zeros_like(ref)

        def dots(lhs_refs):
            accs = []
            for n, (_, ai, _, transposed) in enumerate(w_list):
                w_ref = wb_refs[n] if n in wb_refs else w_refs[n]
                a = lhs_refs[ai][...]
                if transposed:
                    accs.append(lax.dot_general(a, w_ref[...], _NT, preferred_element_type=F32))
                else:
                    accs.append(jnp.dot(a, w_ref[...], preferred_element_type=F32))
            return accs

        accs = dots(a_refs)
        if acc_history:
            for n in range(nw):
                _push_history(acc_refs[n], accs[n])
            accs = list(acc_refs)
        epilogue(accs, e_refs, o_refs, user_scratch, j, i)

        if t_epilogue is not None:
            @pl.when(i == ni - 1)
            def _():
                t_epilogue(dots(ta_refs), te_refs, to_refs, j)

    est = 0
    for a in a_list:
        est += 2 * _nbytes((tm, a.shape[1]), a.dtype)
    for a in t_a:
        est += 2 * _nbytes(a.shape, a.dtype)
    for w, _, _, tr in w_list:
        est += 2 * _nbytes(w_tile_shape(w, tr), w.dtype)
        if w.dtype != BF16:
            est += _nbytes(w_tile_shape(w, tr), BF16)
    for arr, blk, _ in list(extra_list) + list(out_list) + list(t_extra) + list(t_out):
        est += 2 * _nbytes(blk, arr.dtype)
    est += 3 * nw * tm * tn * 4

    outs = pl.pallas_call(
        kernel,
        grid=(nj, ni),
        in_specs=in_specs,
        out_specs=out_specs,
        out_shape=out_shape,
        scratch_shapes=scratch_shapes,
        compiler_params=pltpu.CompilerParams(
            dimension_semantics=("arbitrary", "arbitrary"),
            vmem_limit_bytes=_vmem_limit(est)),
        name=name,
    )(*a_list, *t_a, *[w[0] for w in w_list], *[e for e, _, _ in extra_list],
      *[e for e, _, _ in t_extra])
    return outs[:no], outs[no:]


def _tile(tm, tn):
    return (tm, tn), (lambda j, i: (i, j))


def _tail_tile(rows, tn, off_blocks=0):
    return (rows, tn), (lambda j, i: (0, off_blocks + j))


def _push_history(hist_ref, tile):
    tm = tile.shape[0]
    prev_tail = hist_ref[tm:tm + SUBLANES, :]
    hist_ref[SUBLANES:SUBLANES + tm, :] = tile
    hist_ref[0:SUBLANES, :] = prev_tail


def _conv3_history(hist_ref, seq_start, w3):
    tm = hist_ref.shape[0] - SUBLANES
    hist_ref[0:SUBLANES, :] = jnp.where(seq_start, 0.0, hist_ref[0:SUBLANES, :])
    taps = [hist_ref[SUBLANES - (CONV_W - 1) + t:SUBLANES - (CONV_W - 1) + t + tm, :]
            for t in range(CONV_W)]
    return w3[0:1, :] * taps[0] + w3[1:2, :] * taps[1] + w3[2:3, :] * taps[2]


def _history_tail(hist_ref):
    tm = hist_ref.shape[0] - SUBLANES
    return hist_ref[tm:tm + SUBLANES, :]


def _conv3_step(w3, prev2, prev1, cur):
    return w3[0:1, :] * prev2 + w3[1:2, :] * prev1 + w3[2:3, :] * cur


def _silu(x):
    return x * jax.nn.sigmoid(x)


def _short_conv_branch(n1_q, n1_s, w_in_t, w_conv, state2d, *, seq_len, tm, tn):
    m, ms = n1_q.shape[0], n1_s.shape[0]
    nj = D_CONV // tn
    tps = seq_len // tm
    n_seq = m // seq_len
    w_list = [(w_in_t, 0, OFF_HA, True), (w_in_t, 0, OFF_CA, True), (w_in_t, 0, OFF_BA, True)]
    wc = (w_conv, (CONV_W, tn), lambda j, i: (0, j))

    def epilogue(accs, e_refs, o_refs, s_refs, j, i):
        h, c, b = accs
        (wc_ref,), (xa_ref, last_ref), (u_hist,) = e_refs, o_refs, s_refs
        _push_history(u_hist, c * h)
        cu = _conv3_history(u_hist, i % tps == 0, wc_ref[...])
        xa_ref[...] = (b * cu).astype(BF16)
        last_ref[0] = _history_tail(u_hist)

    def step_epilogue(accs, e_refs, o_refs, j):
        h, c, b = accs
        wc_ref, s0_ref, s1_ref = e_refs
        xa_ref, u_ref = o_refs
        u = c * h
        xa_ref[...] = (b * _conv3_step(wc_ref[...], s0_ref[...], s1_ref[...], u)).astype(BF16)
        u_ref[...] = u

    outs = [(jax.ShapeDtypeStruct((m, D_CONV), BF16),) + _tile(tm, tn),
            (jax.ShapeDtypeStruct((n_seq, SUBLANES, D_CONV), F32), (1, SUBLANES, tn),
             lambda j, i: (i // tps, 0, j))]
    tail = ([n1_s],
            [wc, (state2d,) + _tail_tile(ms, tn), (state2d,) + _tail_tile(ms, tn, nj)],
            [(jax.ShapeDtypeStruct((ms, D_CONV), BF16),) + _tail_tile(ms, tn),
             (jax.ShapeDtypeStruct((ms, D_CONV), F32),) + _tail_tile(ms, tn)],
            step_epilogue)
    return _fused_matmul("short_conv", [n1_q], w_list, [wc], outs, epilogue,
                         tm=tm, tn=tn, nj=nj, scratch=[pltpu.VMEM((SUBLANES + tm, tn), F32)],
                         zero_scratch=True, tail=tail)


def _qkvr_proj(n1_q, n1_s, w_in_t, *, tm, tn):
    m, ms = n1_q.shape[0], n1_s.shape[0]
    n = 2 * GLA_KEY + 2 * GLA_VAL

    def epilogue(accs, e_refs, o_refs, s_refs, j, i):
        o_refs[0][...] = accs[0]

    def step_epilogue(accs, e_refs, o_refs, j):
        o_refs[0][...] = accs[0]

    outs = [(jax.ShapeDtypeStruct((m, n), F32),) + _tile(tm, tn)]
    tail = ([n1_s], [], [(jax.ShapeDtypeStruct((ms, n), F32),) + _tail_tile(ms, tn)],
            step_epilogue)
    (zq,), (zq_s,) = _fused_matmul("qkvr_proj", [n1_q], [(w_in_t, 0, OFF_QKVR, True)], [], outs,
                                   epilogue, tm=tm, tn=tn, nj=n // tn, tail=tail)
    return zq, zq_s


def _log_decay_kernel(n_ref, w1_ref, w2_ref, b_ref, o_ref):
    a_lr = lax.dot_general(n_ref[...], w1_ref[...], _NT, preferred_element_type=F32)
    x = jnp.dot(a_lr.astype(BF16), w2_ref[...], preferred_element_type=F32) + b_ref[...]
    log_sig = jnp.minimum(x, 0.0) - jnp.log1p(jnp.exp(-jnp.abs(x)))
    o_ref[...] = log_sig / GATE_NORM


def _log_decay(n1, w_alr_t, w_alpha2, b_alpha, *, tm):
    m, d = n1.shape
    tm = min(m, tm)
    w1 = jnp.pad(w_alr_t, ((0, LANES - GATE_RANK), (0, 0))).astype(BF16)
    w2 = jnp.pad(w_alpha2, ((0, LANES - GATE_RANK), (0, 0))).astype(BF16)
    return pl.pallas_call(
        _log_decay_kernel,
        grid=(m // tm,),
        in_specs=[pl.BlockSpec((tm, d), lambda i: (i, 0)),
                  pl.BlockSpec((LANES, d), lambda i: (0, 0)),
                  pl.BlockSpec((LANES, GLA_KEY), lambda i: (0, 0)),
                  pl.BlockSpec((1, GLA_KEY), lambda i: (0, 0))],
        out_specs=pl.BlockSpec((tm, GLA_KEY), lambda i: (i, 0)),
        out_shape=jax.ShapeDtypeStruct((m, GLA_KEY), F32),
        compiler_params=pltpu.CompilerParams(
            dimension_semantics=("arbitrary",),
            vmem_limit_bytes=_vmem_limit(2 * tm * d * 2 + 6 * tm * GLA_KEY * 4 + 4 * 2**20)),
        name="log_decay",
    )(n1, w1, w2, b_alpha.reshape(1, GLA_KEY))


def _merge(lhs_q, lhs_s, w_in_t, w_out_conv, w_out_gla, *, tm, tn):
    m, ms = lhs_q[0].shape[0], lhs_s[0].shape[0]

    def merged(accs):
        ga, gb, ya, yb = accs
        return (jax.nn.sigmoid(ga) * ya + jax.nn.sigmoid(gb) * yb).astype(BF16)

    def epilogue(accs, e_refs, o_refs, s_refs, j, i):
        o_refs[0][...] = merged(accs)

    def step_epilogue(accs, e_refs, o_refs, j):
        o_refs[0][...] = merged(accs)

    w_list = [(w_in_t, 0, OFF_GATE_A, True), (w_in_t, 0, OFF_GATE_B, True),
              (w_out_conv, 1, 0, False), (w_out_gla, 2, 0, False)]
    outs = [(jax.ShapeDtypeStruct((m, D_MODEL), BF16),) + _tile(tm, tn)]
    tail = (list(lhs_s), [], [(jax.ShapeDtypeStruct((ms, D_MODEL), BF16),) + _tail_tile(ms, tn)],
            step_epilogue)
    (out,), (out_s,) = _fused_matmul("gated_merge", list(lhs_q), w_list, [], outs, epilogue,
                                     tm=tm, tn=tn, nj=D_MODEL // tn, tail=tail)
    return out, out_s


def _proj_residual(name, a_q, res_q, a_s, res_s, w, *, tm, tn):
    m, ms = a_q.shape[0], a_s.shape[0]
    n = w.shape[1]

    def epilogue(accs, e_refs, o_refs, s_refs, j, i):
        o_refs[0][...] = e_refs[0][...] + accs[0]

    def step_epilogue(accs, e_refs, o_refs, j):
        o_refs[0][...] = e_refs[0][...] + accs[0]

    extras = [(res_q,) + _tile(tm, tn)]
    outs = [(jax.ShapeDtypeStruct((m, n), F32),) + _tile(tm, tn)]
    tail = ([a_s], [(res_s,) + _tail_tile(ms, tn)],
            [(jax.ShapeDtypeStruct((ms, n), F32),) + _tail_tile(ms, tn)], step_epilogue)
    (out,), (out_s,) = _fused_matmul(name, [a_q], [(w, 0, 0, False)], extras, outs, epilogue,
                                     tm=tm, tn=tn, nj=n // tn, tail=tail)
    return out, out_s


def _conv_ffn_up(n2_q, n2_s, w_up, w_ffn_conv, state2d, *, seq_len, tm, tn):
    m, ms = n2_q.shape[0], n2_s.shape[0]
    nj = D_FF // tn
    tps = seq_len // tm
    n_seq = m // seq_len
    w_list = [(w_up, 0, 0, False), (w_up, 0, D_FF, False)]
    wcg = (w_ffn_conv, (CONV_W, tn), lambda j, i: (0, j))
    wcv = (w_ffn_conv, (CONV_W, tn), lambda j, i: (0, nj + j))

    def epilogue(accs, e_refs, o_refs, s_refs, j, i):
        ug_hist, uv_hist = accs
        wg_ref, wv_ref = e_refs
        act_ref, lastg_ref, lastv_ref = o_refs
        seq_start = i % tps == 0
        cg = _conv3_history(ug_hist, seq_start, wg_ref[...])
        cv = _conv3_history(uv_hist, seq_start, wv_ref[...])
        act_ref[...] = (_silu(cg) * cv).astype(BF16)
        lastg_ref[0] = _history_tail(ug_hist)
        lastv_ref[0] = _history_tail(uv_hist)

    def step_epilogue(accs, e_refs, o_refs, j):
        ug, uv = accs
        wg_ref, wv_ref, g0, v0, g1, v1 = e_refs
        act_ref, ug_ref, uv_ref = o_refs
        cg = _conv3_step(wg_ref[...], g0[...], g1[...], ug)
        cv = _conv3_step(wv_ref[...], v0[...], v1[...], uv)
        act_ref[...] = (_silu(cg) * cv).astype(BF16)
        ug_ref[...] = ug
        uv_ref[...] = uv

    last = (jax.ShapeDtypeStruct((n_seq, SUBLANES, D_FF), F32), (1, SUBLANES, tn),
            lambda j, i: (i // tps, 0, j))
    outs = [(jax.ShapeDtypeStruct((m, D_FF), BF16),) + _tile(tm, tn), last, last]
    up_s = (jax.ShapeDtypeStruct((ms, D_FF), F32),) + _tail_tile(ms, tn)
    tail = ([n2_s],
            [wcg, wcv] + [(state2d,) + _tail_tile(ms, tn, q * nj) for q in range(4)],
            [(jax.ShapeDtypeStruct((ms, D_FF), BF16),) + _tail_tile(ms, tn), up_s, up_s],
            step_epilogue)
    return _fused_matmul("conv_ffn_up", [n2_q], w_list, [wcg, wcv], outs, epilogue,
                         tm=tm, tn=tn, nj=nj, acc_history=True, tail=tail)


def _ple(lhs_q, h2_q, lhs_s, h2_s, w_ple_gate, w_ple_proj, *, tm, tn):
    m, ms = h2_q.shape[0], h2_s.shape[0]

    def epilogue(accs, e_refs, o_refs, s_refs, j, i):
        gate, proj = accs
        o_refs[0][...] = e_refs[0][...] + jax.nn.sigmoid(gate) * proj

    def step_epilogue(accs, e_refs, o_refs, j):
        gate, proj = accs
        o_refs[0][...] = e_refs[0][...] + jax.nn.sigmoid(gate) * proj

    extras = [(h2_q,) + _tile(tm, tn)]
    outs = [(jax.ShapeDtypeStruct((m, D_MODEL), F32),) + _tile(tm, tn)]
    tail = (list(lhs_s), [(h2_s,) + _tail_tile(ms, tn)],
            [(jax.ShapeDtypeStruct((ms, D_MODEL), F32),) + _tail_tile(ms, tn)], step_epilogue)
    (out,), (out_s,) = _fused_matmul(
        "ple_gate", list(lhs_q), [(w_ple_gate, 0, 0, False), (w_ple_proj, 1, 0, False)], extras,
        outs, epilogue, tm=tm, tn=tn, nj=D_MODEL // tn, tail=tail)
    return out, out_s


def _gla_seq_kernel(q_ref, k_ref, v_ref, r_ref, la_ref, g_ref, xb_ref, sfin_ref,
                    st_ref, b_sc, qt_sc, ks_sc, v_sc, oi_sc, *, seq_len):
    sub, c = GLA_SUB, GLA_CHUNK
    n_chunks = sub // c
    st_ref[...] = jnp.zeros_like(st_ref)
    row = lax.broadcasted_iota(jnp.int32, (sub, sub), 0)
    col = lax.broadcasted_iota(jnp.int32, (sub, sub), 1)
    row_in_chunk = row % c
    causal = ((row // c) == (col // c)) & (col <= row)

    def sub_block(sb, carry):
        r0 = pl.multiple_of(sb * sub, sub)
        b = la_ref[pl.ds(r0, sub), :]
        s = 1
        while s < c:
            b = b + jnp.where(row_in_chunk >= s, pltpu.roll(b, s, axis=0), 0.0)
            s *= 2
        b_sc[...] = b
        q = q_ref[pl.ds(r0, sub), :] * (GLA_DK ** -0.5)
        k = k_ref[pl.ds(r0, sub), :]
        qt = (q * jnp.exp(b)).astype(BF16)
        kt = (k * jnp.exp(-b)).astype(BF16)
        vb = v_ref[pl.ds(r0, sub), :].astype(BF16)
        qt_sc[...] = qt
        v_sc[...] = vb
        decays = []
        for n in range(n_chunks):
            b_last = b_sc[n * c + c - 1:n * c + c, :]
            ks_n = k[n * c:(n + 1) * c, :] * jnp.exp(b_last - b[n * c:(n + 1) * c, :])
            ks_sc[n * c:(n + 1) * c, :] = ks_n.astype(BF16)
            decays.append(jnp.exp(b_last))
        scores = lax.dot_general(qt, kt, _NT, preferred_element_type=F32)
        scores = jnp.where(causal, scores, 0.0).astype(BF16)
        o_intra = jnp.dot(scores, vb, preferred_element_type=F32)
        for n in range(n_chunks):
            rows = slice(n * c, (n + 1) * c)
            st = st_ref[...]
            oi_sc[rows, :] = lax.dot_general(qt_sc[rows, :], st.astype(BF16), _NT,
                                             preferred_element_type=F32)
            upd = lax.dot_general(v_sc[rows, :], ks_sc[rows, :], _TN, preferred_element_type=F32)
            st_ref[...] = st * decays[n] + upd
        o = o_intra + oi_sc[...]
        ms = jnp.mean(o * o, axis=-1, keepdims=True)
        y = (o * lax.rsqrt(ms + EPS)) * g_ref[...]
        r = r_ref[pl.ds(r0, sub), :]
        xb_ref[pl.ds(r0, sub), :] = (y * _silu(r)).astype(BF16)
        return carry

    lax.fori_loop(0, seq_len // sub, sub_block, 0)
    sfin_ref[0, 0] = st_ref[...].T


def _gla_seq(zq, log_a, g_gla, *, n_seq, seq_len):
    h = GLA_HEADS
    blk = (seq_len, GLA_DK)

    def col(off):
        return pl.BlockSpec(blk, functools.partial(lambda b, hh, off: (b, off + hh), off=off))

    est = 2 * 5 * _nbytes(blk, F32) + 2 * _nbytes(blk, BF16) + 16 * GLA_SUB * GLA_SUB * 4
    return pl.pallas_call(
        functools.partial(_gla_seq_kernel, seq_len=seq_len),
        grid=(n_seq, h),
        in_specs=[col(0), col(h), col(2 * h), col(3 * h),
                  pl.BlockSpec(blk, lambda b, hh: (b, hh)),
                  pl.BlockSpec((1, GLA_DV), lambda b, hh: (0, 0))],
        out_specs=[pl.BlockSpec(blk, lambda b, hh: (b, hh)),
                   pl.BlockSpec((1, 1, GLA_DK, GLA_DV), lambda b, hh: (b, hh, 0, 0))],
        out_shape=[jax.ShapeDtypeStruct((n_seq * seq_len, GLA_VAL), BF16),
                   jax.ShapeDtypeStruct((n_seq, h, GLA_DK, GLA_DV), F32)],
        scratch_shapes=[pltpu.VMEM((GLA_DV, GLA_DK), F32),
                        pltpu.VMEM((GLA_SUB, GLA_DK), F32),
                        pltpu.VMEM((GLA_SUB, GLA_DK), BF16),
                        pltpu.VMEM((GLA_SUB, GLA_DK), BF16),
                        pltpu.VMEM((GLA_SUB, GLA_DV), BF16),
                        pltpu.VMEM((GLA_SUB, GLA_DV), F32)],
        compiler_params=pltpu.CompilerParams(
            dimension_semantics=("arbitrary", "arbitrary"),
            vmem_limit_bytes=_vmem_limit(est)),
        name="gla_seq",
    )(zq, zq, zq, zq, log_a, g_gla.reshape(1, GLA_DV))


def _to_column(x_row, eye):
    return jnp.sum(jnp.where(eye, x_row, 0.0), axis=1, keepdims=True)


def _gla_step_kernel(zq_ref, la_ref, g_ref, s0_ref, xb_ref, snew_ref):
    eye = (lax.broadcasted_iota(jnp.int32, (GLA_DK, GLA_DK), 0)
           == lax.broadcasted_iota(jnp.int32, (GLA_DK, GLA_DK), 1))
    pad_rows = 2 * SUBLANES
    for h in range(GLA_HEADS):
        def cols(base, width=GLA_DK, h=h):
            return slice(base + h * width, base + (h + 1) * width)
        q = zq_ref[0, :, cols(0)] * (GLA_DK ** -0.5)
        k = zq_ref[0, :, cols(GLA_KEY)]
        v = zq_ref[0, :, cols(2 * GLA_KEY)]
        r = zq_ref[0, :, cols(2 * GLA_KEY + GLA_VAL)]
        b = la_ref[0, :, cols(0)]
        qt = q * jnp.exp(b)
        kt = k * jnp.exp(-b)
        a = jnp.exp(b)
        s0 = s0_ref[0, h]
        score = jnp.sum(qt * kt, axis=-1, keepdims=True)
        o_inter = jnp.dot(jnp.broadcast_to(qt, (pad_rows, GLA_DK)).astype(BF16), s0.astype(BF16),
                          preferred_element_type=F32)[0:1, :]
        o = score * v + o_inter
        snew_ref[0, h] = s0 * _to_column(a, eye) + _to_column(k, eye) * v
        ms = jnp.mean(o * o, axis=-1, keepdims=True)
        y = (o * lax.rsqrt(ms + EPS)) * g_ref[...]
        xb_ref[0, :, cols(0, GLA_DV)] = (y * _silu(r)).astype(BF16)


def _gla_step(zq, log_a, g_gla, s0):
    n = zq.shape[0]
    wq = zq.shape[1]
    sblk = (1, GLA_HEADS, GLA_DK, GLA_DV)
    est = 4 * _nbytes(sblk, F32) + 8 * GLA_DK * GLA_DV * 4
    xb, s_new = pl.pallas_call(
        _gla_step_kernel,
        grid=(n,),
        in_specs=[pl.BlockSpec((1, 1, wq), lambda b: (b, 0, 0)),
                  pl.BlockSpec((1, 1, GLA_KEY), lambda b: (b, 0, 0)),
                  pl.BlockSpec((1, GLA_DV), lambda b: (0, 0)),
                  pl.BlockSpec(sblk, lambda b: (b, 0, 0, 0))],
        out_specs=[pl.BlockSpec((1, 1, GLA_VAL), lambda b: (b, 0, 0)),
                   pl.BlockSpec(sblk, lambda b: (b, 0, 0, 0))],
        out_shape=[jax.ShapeDtypeStruct((n, 1, GLA_VAL), BF16),
                   jax.ShapeDtypeStruct(s0.shape, F32)],
        compiler_params=pltpu.CompilerParams(
            dimension_semantics=("arbitrary",),
            vmem_limit_bytes=_vmem_limit(est)),
        name="gla_step",
    )(zq.reshape(n, 1, wq), log_a.reshape(n, 1, GLA_KEY), g_gla.reshape(1, GLA_DV), s0)
    return xb.reshape(n, GLA_VAL), s_new


_PLAN = dict(
    norm=512,
    short_conv=dict(tm=512, tn=256),
    qkvr=dict(tm=1024, tn=512),
    log_decay=512,
    merge=dict(tm=512, tn=256),
    mix_out=dict(tm=1024, tn=512),
    ffn_up=dict(tm=1024, tn=256),
    ffn_down=dict(tm=512, tn=512),
    ple=dict(tm=1024, tn=512),
)


def kernel(x_prompt, x_sample, p_prompt, p_sample, state_conv, state_gla, state_ffn, g_mix, w_in,
           w_alpha2, b_alpha, w_conv, w_out_conv, g_gla, w_out_gla, w_mix_out, g_ffn, w_up,
           w_ffn_conv, w_down, g_ple, w_ple_gate, w_ple_proj, g_final):
    assert g_mix.shape[0] == 1, "single-layer trunk"
    assert x_sample.shape[1] == 1, "the sample group advances one token per sequence"
    n_seq, seq_len, d = x_prompt.shape
    n_step = x_sample.shape[0]
    plan = _PLAN
    w_in_t = jnp.swapaxes(w_in[0], 0, 1)
    w_alr_t = w_in_t[OFF_ALR:OFF_GATE_A]
    w_down_b = w_down[0].astype(BF16)
    conv_state, gla_state, ffn_state = state_conv[0], state_gla[0], state_ffn[0]

    h0_q = x_prompt.reshape(n_seq * seq_len, d)
    h0_s = x_sample.reshape(n_step, d)
    p_q = p_prompt[0].reshape(n_seq * seq_len, -1).astype(BF16)
    p_s = p_sample[0].reshape(n_step, -1).astype(BF16)

    n1_q = _rmsnorm(h0_q, g_mix[0], BF16, plan["norm"])
    n1_s = _rmsnorm(h0_s, g_mix[0], BF16, plan["norm"])
    zq_q, zq_s = _qkvr_proj(n1_q, n1_s, w_in_t, **plan["qkvr"])
    la_q = _log_decay(n1_q, w_alr_t, w_alpha2[0], b_alpha[0], tm=plan["log_decay"])
    la_s = _log_decay(n1_s, w_alr_t, w_alpha2[0], b_alpha[0], tm=plan["log_decay"])
    (xa_q, conv_last), (xa_s, u_s) = _short_conv_branch(
        n1_q, n1_s, w_in_t, w_conv[0], conv_state.reshape(n_step, -1), seq_len=seq_len,
        **plan["short_conv"])
    xb_q, gla_q = _gla_seq(zq_q, la_q, g_gla[0], n_seq=n_seq, seq_len=seq_len)
    xb_s, gla_s = _gla_step(zq_s, la_s, g_gla[0], gla_state)

    mg_q, mg_s = _merge([n1_q, xa_q, xb_q], [n1_s, xa_s, xb_s], w_in_t, w_out_conv[0],
                        w_out_gla[0], **plan["merge"])
    h1_q, h1_s = _proj_residual("mix_out", mg_q, h0_q, mg_s, h0_s, w_mix_out[0],
                                **plan["mix_out"])

    n2_q = _rmsnorm(h1_q, g_ffn[0], BF16, plan["norm"])
    n2_s = _rmsnorm(h1_s, g_ffn[0], BF16, plan["norm"])
    (act_q, last_g, last_v), (act_s, ug_s, uv_s) = _conv_ffn_up(
        n2_q, n2_s, w_up[0], w_ffn_conv[0], ffn_state.reshape(n_step, -1), seq_len=seq_len,
        **plan["ffn_up"])
    h2_q, h2_s = _proj_residual("ffn_down", act_q, h1_q, act_s, h1_s, w_down_b,
                                **plan["ffn_down"])

    n3_q = _rmsnorm(h2_q, g_ple[0], BF16, plan["norm"])
    n3_s = _rmsnorm(h2_s, g_ple[0], BF16, plan["norm"])
    h3_q, h3_s = _ple([n3_q, p_q], h2_q, [n3_s, p_s], h2_s, w_ple_gate[0], w_ple_proj[0],
                      **plan["ple"])
    y_q = _rmsnorm(h3_q, g_final, F32, plan["norm"]).reshape(n_seq, seq_len, d)
    y_s = _rmsnorm(h3_s, g_final, F32, plan["norm"]).reshape(n_step, 1, d)

    keep = slice(SUBLANES - (CONV_W - 1), SUBLANES)
    conv_q = conv_last[:, keep, :]
    ffn_q = jnp.concatenate([last_g, last_v], axis=-1)[:, keep, :]
    conv_s = jnp.stack([conv_state[:, 1, :], u_s], axis=1)
    ffn_s = jnp.stack([ffn_state[:, 1, :], jnp.concatenate([ug_s, uv_s], axis=-1)], axis=1)
    return (y_q, y_s, conv_q[None], conv_s[None], gla_q[None], gla_s[None], ffn_q[None],
            ffn_s[None])
```

```python
import functools

import jax
import jax.numpy as jnp
from jax import lax
from jax.experimental import pallas as pl
from jax.experimental.pallas import tpu as pltpu

F32 = jnp.float32
BF16 = jnp.bfloat16

D_MODEL = 4096
D_CONV = 2048
CONV_W = 3
GLA_HEADS = 8
GLA_DK = 256
GLA_DV = 256
GLA_KEY = GLA_HEADS * GLA_DK
GLA_VAL = GLA_HEADS * GLA_DV
GATE_RANK = 16
GATE_NORM = 16.0
GLA_CHUNK = 32
D_FF = 11008
EPS = 1e-6

V7X_VMEM_BYTES = 64 * 2**20
V7X_VMEM_CAP = 60 * 2**20
SUBLANES = 8
LANES = 128
MXU_DIM = 256
GLA_SUB = MXU_DIM
CAST_CHUNK_ELEMS = 256 * 1024

OFF_HA = 0
OFF_CA = D_CONV
OFF_BA = 2 * D_CONV
OFF_QKVR = 3 * D_CONV
OFF_ALR = OFF_QKVR + 2 * GLA_KEY + 2 * GLA_VAL
OFF_GATE_A = OFF_ALR + GATE_RANK
OFF_GATE_B = OFF_GATE_A + D_MODEL

_NT = (((1,), (1,)), ((), ()))
_TN = (((0,), (0,)), ((), ()))


def _vmem_limit(nbytes):
    return int(min(V7X_VMEM_CAP, nbytes + 8 * 2**20))


def _nbytes(shape, dtype):
    n = 1
    for s in shape:
        n *= int(s)
    return n * jnp.dtype(dtype).itemsize


def _largest_divisor(n, limit, multiple):
    best = multiple
    for d in range(multiple, min(n, limit) + 1, multiple):
        if n % d == 0:
            best = d
    assert n % best == 0
    return best


def _rmsnorm_kernel(x_ref, g_ref, o_ref):
    x = x_ref[...]
    ms = jnp.mean(x * x, axis=-1, keepdims=True)
    o_ref[...] = ((x * lax.rsqrt(ms + EPS)) * g_ref[...]).astype(o_ref.dtype)


def _rmsnorm(x, g, out_dtype, tm):
    m, d = x.shape
    tm = min(m, tm)
    return pl.pallas_call(
        _rmsnorm_kernel,
        grid=(m // tm,),
        in_specs=[pl.BlockSpec((tm, d), lambda i: (i, 0)),
                  pl.BlockSpec((1, d), lambda i: (0, 0))],
        out_specs=pl.BlockSpec((tm, d), lambda i: (i, 0)),
        out_shape=jax.ShapeDtypeStruct((m, d), out_dtype),
        compiler_params=pltpu.CompilerParams(
            dimension_semantics=("arbitrary",),
            vmem_limit_bytes=_vmem_limit(6 * tm * d * 4)),
        name="rmsnorm",
    )(x, g.reshape(1, d))


def _cast_rows(src_ref, dst_ref):
    rows, cols = src_ref.shape
    ck = _largest_divisor(rows, max(2 * SUBLANES, CAST_CHUNK_ELEMS // cols), 2 * SUBLANES)

    def body(c, carry):
        r = pl.multiple_of(c * ck, ck)
        dst_ref[pl.ds(r, ck), :] = src_ref[pl.ds(r, ck), :].astype(BF16)
        return carry

    lax.fori_loop(0, rows // ck, body, 0)


def _fused_matmul(name, a_list, w_list, extra_list, out_list, epilogue, *, tm, tn, nj,
                  scratch=(), acc_history=False, zero_scratch=False, tail=None):
    m = a_list[0].shape[0]
    assert m % tm == 0
    ni = m // tm
    t_a, t_extra, t_out, t_epilogue = tail if tail is not None else ([], [], [], None)
    assert len(t_a) in (0, len(a_list))
    na, nw, ne, no = len(a_list), len(w_list), len(extra_list), len(out_list)
    nta, nte, nto = len(t_a), len(t_extra), len(t_out)
    cast_ids = [n for n, w in enumerate(w_list) if w[0].dtype != BF16]

    def w_block(w, off, transposed):
        if transposed:
            assert off % SUBLANES == 0 and tn % SUBLANES == 0
            return ((pl.Element(tn), pl.Element(w.shape[1])),
                    (lambda j, i: (pl.multiple_of(off + j * tn, SUBLANES), 0)))
        assert off % tn == 0
        return (w.shape[0], tn), (lambda j, i: (0, off // tn + j))

    def w_tile_shape(w, transposed):
        return (tn, w.shape[1]) if transposed else (w.shape[0], tn)

    in_specs = [pl.BlockSpec((tm, a.shape[1]), lambda j, i: (i, 0)) for a in a_list]
    in_specs += [pl.BlockSpec(a.shape, lambda j, i: (0, 0)) for a in t_a]
    in_specs += [pl.BlockSpec(*w_block(w, off, tr)) for w, _, off, tr in w_list]
    in_specs += [pl.BlockSpec(blk, im) for _, blk, im in list(extra_list) + list(t_extra)]
    out_specs = [pl.BlockSpec(blk, im) for _, blk, im in list(out_list) + list(t_out)]
    out_shape = [s for s, _, _ in list(out_list) + list(t_out)]
    scratch_shapes = [pltpu.VMEM(w_tile_shape(w_list[n][0], w_list[n][3]), BF16) for n in cast_ids]
    n_acc = nw if acc_history else 0
    scratch_shapes += [pltpu.VMEM((SUBLANES + tm, tn), F32)] * n_acc
    scratch_shapes += list(scratch)

    def kernel(*refs):
        pos = 0

        def take(count):
            nonlocal pos
            pos += count
            return refs[pos - count:pos]

        a_refs, ta_refs, w_refs = take(na), take(nta), take(nw)
        e_refs, te_refs, o_refs, to_refs = take(ne), take(nte), take(no), take(nto)
        s_refs = refs[pos:]
        wb_refs = dict(zip(cast_ids, s_refs[:len(cast_ids)]))
        acc_refs = s_refs[len(cast_ids):len(cast_ids) + n_acc]
        user_scratch = s_refs[len(cast_ids) + n_acc:]
        j = pl.program_id(0)
        i = pl.program_id(1)

        @pl.when(i == 0)
        def _():
            for n in cast_ids:
                _cast_rows(w_refs[n], wb_refs[n])
            for ref in list(acc_refs) + (list(user_scratch) if zero_scratch else []):
                ref[...] = jnp.zeros_like(ref)

        def dots(lhs_refs):
            accs = []
            for n, (_, ai, _, transposed) in enumerate(w_list):
                w_ref = wb_refs[n] if n in wb_refs else w_refs[n]
                a = lhs_refs[ai][...]
                if transposed:
                    accs.append(lax.dot_general(a, w_ref[...], _NT, preferred_element_type=F32))
                else:
                    accs.append(jnp.dot(a, w_ref[...], preferred_element_type=F32))
            return accs

        accs = dots(a_refs)
        if acc_history:
            for n in range(nw):
                _push_history(acc_refs[n], accs[n])
            accs = list(acc_refs)
        epilogue(accs, e_refs, o_refs, user_scratch, j, i)

        if t_epilogue is not None:
            @pl.when(i == ni - 1)
            def _():
                t_epilogue(dots(ta_refs), te_refs, to_refs, j)

    est = 0
    for a in a_list:
        est += 2 * _nbytes((tm, a.shape[1]), a.dtype)
    for a in t_a:
        est += 2 * _nbytes(a.shape, a.dtype)
    for w, _, _, tr in w_list:
        est += 2 * _nbytes(w_tile_shape(w, tr), w.dtype)
        if w.dtype != BF16:
            est += _nbytes(w_tile_shape(w, tr), BF16)
    for arr, blk, _ in list(extra_list) + list(out_list) + list(t_extra) + list(t_out):
        est += 2 * _nbytes(blk, arr.dtype)
    est += 3 * nw * tm * tn * 4

    outs = pl.pallas_call(
        kernel,
        grid=(nj, ni),
        in_specs=in_specs,
        out_specs=out_specs,
        out_shape=out_shape,
        scratch_shapes=scratch_shapes,
        compiler_params=pltpu.CompilerParams(
            dimension_semantics=("arbitrary", "arbitrary"),
            vmem_limit_bytes=_vmem_limit(est)),
        name=name,
    )(*a_list, *t_a, *[w[0] for w in w_list], *[e for e, _, _ in extra_list],
      *[e for e, _, _ in t_extra])
    return outs[:no], outs[no:]


def _tile(tm, tn):
    return (tm, tn), (lambda j, i: (i, j))


def _tail_tile(rows, tn, off_blocks=0):
    return (rows, tn), (lambda j, i: (0, off_blocks + j))


def _push_history(hist_ref, tile):
    tm = tile.shape[0]
    prev_tail = hist_ref[tm:tm + SUBLANES, :]
    hist_ref[SUBLANES:SUBLANES + tm, :] = tile
    hist_ref[0:SUBLANES, :] = prev_tail


def _conv3_history(hist_ref, seq_start, w3):
    tm = hist_ref.shape[0] - SUBLANES
    hist_ref[0:SUBLANES, :] = jnp.where(seq_start, 0.0, hist_ref[0:SUBLANES, :])
    taps = [hist_ref[SUBLANES - (CONV_W - 1) + t:SUBLANES - (CONV_W - 1) + t + tm, :]
            for t in range(CONV_W)]
    return w3[0:1, :] * taps[0] + w3[1:2, :] * taps[1] + w3[2:3, :] * taps[2]


def _history_tail(hist_ref):
    tm = hist_ref.shape[0] - SUBLANES
    return hist_ref[tm:tm + SUBLANES, :]


def _conv3_step(w3, prev2, prev1, cur):
    return w3[0:1, :] * prev2 + w3[1:2, :] * prev1 + w3[2:3, :] * cur


def _silu(x):
    return x * jax.nn.sigmoid(x)


def _short_conv_branch(n1_q, n1_s, w_in_t, w_conv, state2d, *, seq_len, tm, tn):
    m, ms = n1_q.shape[0], n1_s.shape[0]
    nj = D_CONV // tn
    tps = seq_len // tm
    n_seq = m // seq_len
    w_list = [(w_in_t, 0, OFF_HA, True), (w_in_t, 0, OFF_CA, True), (w_in_t, 0, OFF_BA, True)]
    wc = (w_conv, (CONV_W, tn), lambda j, i: (0, j))

    def epilogue(accs, e_refs, o_refs, s_refs, j, i):
        h, c, b = accs
        (wc_ref,), (xa_ref, last_ref), (u_hist,) = e_refs, o_refs, s_refs
        _push_history(u_hist, c * h)
        cu = _conv3_history(u_hist, i % tps == 0, wc_ref[...])
        xa_ref[...] = (b * cu).astype(BF16)
        last_ref[0] = _history_tail(u_hist)

    def step_epilogue(accs, e_refs, o_refs, j):
        h, c, b = accs
        wc_ref, s0_ref, s1_ref = e_refs
        xa_ref, u_ref = o_refs
        u = c * h
        xa_ref[...] = (b * _conv3_step(wc_ref[...], s0_ref[...], s1_ref[...], u)).astype(BF16)
        u_ref[...] = u

    outs = [(jax.ShapeDtypeStruct((m, D_CONV), BF16),) + _tile(tm, tn),
            (jax.ShapeDtypeStruct((n_seq, SUBLANES, D_CONV), F32), (1, SUBLANES, tn),
             lambda j, i: (i // tps, 0, j))]
    tail = ([n1_s],
            [wc, (state2d,) + _tail_tile(ms, tn), (state2d,) + _tail_tile(ms, tn, nj)],
            [(jax.ShapeDtypeStruct((ms, D_CONV), BF16),) + _tail_tile(ms, tn),
             (jax.ShapeDtypeStruct((ms, D_CONV), F32),) + _tail_tile(ms, tn)],
            step_epilogue)
    return _fused_matmul("short_conv", [n1_q], w_list, [wc], outs, epilogue,
                         tm=tm, tn=tn, nj=nj, scratch=[pltpu.VMEM((SUBLANES + tm, tn), F32)],
                         zero_scratch=True, tail=tail)


def _qkvr_proj(n1_q, n1_s, w_in_t, *, tm, tn):
    m, ms = n1_q.shape[0], n1_s.shape[0]
    n = 2 * GLA_KEY + 2 * GLA_VAL

    def epilogue(accs, e_refs, o_refs, s_refs, j, i):
        o_refs[0][...] = accs[0]

    def step_epilogue(accs, e_refs, o_refs, j):
        o_refs[0][...] = accs[0]

    outs = [(jax.ShapeDtypeStruct((m, n), F32),) + _tile(tm, tn)]
    tail = ([n1_s], [], [(jax.ShapeDtypeStruct((ms, n), F32),) + _tail_tile(ms, tn)],
            step_epilogue)
    (zq,), (zq_s,) = _fused_matmul("qkvr_proj", [n1_q], [(w_in_t, 0, OFF_QKVR, True)], [], outs,
                                   epilogue, tm=tm, tn=tn, nj=n // tn, tail=tail)
    return zq, zq_s


def _log_decay_kernel(n_ref, w1_ref, w2_ref, b_ref, o_ref):
    a_lr = lax.dot_general(n_ref[...], w1_ref[...], _NT, preferred_element_type=F32)
    x = jnp.dot(a_lr.astype(BF16), w2_ref[...], preferred_element_type=F32) + b_ref[...]
    log_sig = jnp.minimum(x, 0.0) - jnp.log1p(jnp.exp(-jnp.abs(x)))
    o_ref[...] = log_sig / GATE_NORM


def _log_decay(n1, w_alr_t, w_alpha2, b_alpha, *, tm):
    m, d = n1.shape
    tm = min(m, tm)
    w1 = jnp.pad(w_alr_t, ((0, LANES - GATE_RANK), (0, 0))).astype(BF16)
    w2 = jnp.pad(w_alpha2, ((0, LANES - GATE_RANK), (0, 0))).astype(BF16)
    return pl.pallas_call(
        _log_decay_kernel,
        grid=(m // tm,),
        in_specs=[pl.BlockSpec((tm, d), lambda i: (i, 0)),
                  pl.BlockSpec((LANES, d), lambda i: (0, 0)),
                  pl.BlockSpec((LANES, GLA_KEY), lambda i: (0, 0)),
                  pl.BlockSpec((1, GLA_KEY), lambda i: (0, 0))],
        out_specs=pl.BlockSpec((tm, GLA_KEY), lambda i: (i, 0)),
        out_shape=jax.ShapeDtypeStruct((m, GLA_KEY), F32),
        compiler_params=pltpu.CompilerParams(
            dimension_semantics=("arbitrary",),
            vmem_limit_bytes=_vmem_limit(2 * tm * d * 2 + 6 * tm * GLA_KEY * 4 + 4 * 2**20)),
        name="log_decay",
    )(n1, w1, w2, b_alpha.reshape(1, GLA_KEY))


def _merge(lhs_q, lhs_s, w_in_t, w_out_conv, w_out_gla, *, tm, tn):
    m, ms = lhs_q[0].shape[0], lhs_s[0].shape[0]

    def merged(accs):
        ga, gb, ya, yb = accs
        return (jax.nn.sigmoid(ga) * ya + jax.nn.sigmoid(gb) * yb).astype(BF16)

    def epilogue(accs, e_refs, o_refs, s_refs, j, i):
        o_refs[0][...] = merged(accs)

    def step_epilogue(accs, e_refs, o_refs, j):
        o_refs[0][...] = merged(accs)

    w_list = [(w_in_t, 0, OFF_GATE_A, True), (w_in_t, 0, OFF_GATE_B, True),
              (w_out_conv, 1, 0, False), (w_out_gla, 2, 0, False)]
    outs = [(jax.ShapeDtypeStruct((m, D_MODEL), BF16),) + _tile(tm, tn)]
    tail = (list(lhs_s), [], [(jax.ShapeDtypeStruct((ms, D_MODEL), BF16),) + _tail_tile(ms, tn)],
            step_epilogue)
    (out,), (out_s,) = _fused_matmul("gated_merge", list(lhs_q), w_list, [], outs, epilogue,
                                     tm=tm, tn=tn, nj=D_MODEL // tn, tail=tail)
    return out, out_s


def _proj_residual(name, a_q, res_q, a_s, res_s, w, *, tm, tn):
    m, ms = a_q.shape[0], a_s.shape[0]
    n = w.shape[1]

    def epilogue(accs, e_refs, o_refs, s_refs, j, i):
        o_refs[0][...] = e_refs[0][...] + accs[0]

    def step_epilogue(accs, e_refs, o_refs, j):
        o_refs[0][...] = e_refs[0][...] + accs[0]

    extras = [(res_q,) + _tile(tm, tn)]
    outs = [(jax.ShapeDtypeStruct((m, n), F32),) + _tile(tm, tn)]
    tail = ([a_s], [(res_s,) + _tail_tile(ms, tn)],
            [(jax.ShapeDtypeStruct((ms, n), F32),) + _tail_tile(ms, tn)], step_epilogue)
    (out,), (out_s,) = _fused_matmul(name, [a_q], [(w, 0, 0, False)], extras, outs, epilogue,
                                     tm=tm, tn=tn, nj=n // tn, tail=tail)
    return out, out_s


def _conv_ffn_up(n2_q, n2_s, w_up, w_ffn_conv, state2d, *, seq_len, tm, tn):
    m, ms = n2_q.shape[0], n2_s.shape[0]
    nj = D_FF // tn
    tps = seq_len // tm
    n_seq = m // seq_len
    w_list = [(w_up, 0, 0, False), (w_up, 0, D_FF, False)]
    wcg = (w_ffn_conv, (CONV_W, tn), lambda j, i: (0, j))
    wcv = (w_ffn_conv, (CONV_W, tn), lambda j, i: (0, nj + j))

    def epilogue(accs, e_refs, o_refs, s_refs, j, i):
        ug_hist, uv_hist = accs
        wg_ref, wv_ref = e_refs
        act_ref, lastg_ref, lastv_ref = o_refs
        seq_start = i % tps == 0
        cg = _conv3_history(ug_hist, seq_start, wg_ref[...])
        cv = _conv3_history(uv_hist, seq_start, wv_ref[...])
        act_ref[...] = (_silu(cg) * cv).astype(BF16)
        lastg_ref[0] = _history_tail(ug_hist)
        lastv_ref[0] = _history_tail(uv_hist)

    def step_epilogue(accs, e_refs, o_refs, j):
        ug, uv = accs
        wg_ref, wv_ref, g0, v0, g1, v1 = e_refs
        act_ref, ug_ref, uv_ref = o_refs
        cg = _conv3_step(wg_ref[...], g0[...], g1[...], ug)
        cv = _conv3_step(wv_ref[...], v0[...], v1[...], uv)
        act_ref[...] = (_silu(cg) * cv).astype(BF16)
        ug_ref[...] = ug
        uv_ref[...] = uv

    last = (jax.ShapeDtypeStruct((n_seq, SUBLANES, D_FF), F32), (1, SUBLANES, tn),
            lambda j, i: (i // tps, 0, j))
    outs = [(jax.ShapeDtypeStruct((m, D_FF), BF16),) + _tile(tm, tn), last, last]
    up_s = (jax.ShapeDtypeStruct((ms, D_FF), F32),) + _tail_tile(ms, tn)
    tail = ([n2_s],
            [wcg, wcv] + [(state2d,) + _tail_tile(ms, tn, q * nj) for q in range(4)],
            [(jax.ShapeDtypeStruct((ms, D_FF), BF16),) + _tail_tile(ms, tn), up_s, up_s],
            step_epilogue)
    return _fused_matmul("conv_ffn_up", [n2_q], w_list, [wcg, wcv], outs, epilogue,
                         tm=tm, tn=tn, nj=nj, acc_history=True, tail=tail)


def _ple(lhs_q, h2_q, lhs_s, h2_s, w_ple_gate, w_ple_proj, *, tm, tn):
    m, ms = h2_q.shape[0], h2_s.shape[0]

    def epilogue(accs, e_refs, o_refs, s_refs, j, i):
        gate, proj = accs
        o_refs[0][...] = e_refs[0][...] + jax.nn.sigmoid(gate) * proj

    def step_epilogue(accs, e_refs, o_refs, j):
        gate, proj = accs
        o_refs[0][...] = e_refs[0][...] + jax.nn.sigmoid(gate) * proj

    extras = [(h2_q,) + _tile(tm, tn)]
    outs = [(jax.ShapeDtypeStruct((m, D_MODEL), F32),) + _tile(tm, tn)]
    tail = (list(lhs_s), [(h2_s,) + _tail_tile(ms, tn)],
            [(jax.ShapeDtypeStruct((ms, D_MODEL), F32),) + _tail_tile(ms, tn)], step_epilogue)
    (out,), (out_s,) = _fused_matmul(
        "ple_gate", list(lhs_q), [(w_ple_gate, 0, 0, False), (w_ple_proj, 1, 0, False)], extras,
        outs, epilogue, tm=tm, tn=tn, nj=D_MODEL // tn, tail=tail)
    return out, out_s


def _gla_seq_kernel(q_ref, k_ref, v_ref, r_ref, la_ref, g_ref, xb_ref, sfin_ref,
                    st_ref, b_sc, cum_sc, *, seq_len):
    sub, c = GLA_SUB, GLA_CHUNK
    n_chunks = sub // c
    st_ref[...] = jnp.zeros_like(st_ref)
    row = lax.broadcasted_iota(jnp.int32, (sub, sub), 0)
    col = lax.broadcasted_iota(jnp.int32, (sub, sub), 1)
    row_in_chunk = row % c
    groups = [g for g in (2 * c, 4 * c, 8 * c) if g <= sub]

    def sub_block(sb, carry):
        r0 = pl.multiple_of(sb * sub, sub)
        b = la_ref[pl.ds(r0, sub), :]
        s = 1
        while s < c:
            b = b + jnp.where(row_in_chunk >= s, pltpu.roll(b, s, axis=0), 0.0)
            s *= 2
        b_sc[...] = b
        offset = jnp.zeros((1, GLA_DK), F32)
        for n in range(n_chunks):
            cum_sc[n * c:(n + 1) * c, :] = b[n * c:(n + 1) * c, :] + offset
            offset = offset + b_sc[n * c + c - 1:n * c + c, :]
        cum = cum_sc[...]
        cum_last = offset

        q = q_ref[pl.ds(r0, sub), :] * (GLA_DK ** -0.5)
        k = k_ref[pl.ds(r0, sub), :]
        vb = v_ref[pl.ds(r0, sub), :].astype(BF16)

        qt = (q * jnp.exp(b)).astype(BF16)
        kt = (k * jnp.exp(-b)).astype(BF16)
        diag = ((row // c) == (col // c)) & (col <= row)
        scores = jnp.where(diag, lax.dot_general(qt, kt, _NT, preferred_element_type=F32), 0.0)
        for g in groups:
            half = g // 2
            mid = jnp.concatenate(
                [jnp.broadcast_to(cum_sc[grp * g + half - 1:grp * g + half, :], (g, GLA_DK))
                 for grp in range(sub // g)], axis=0)
            factor = jnp.exp(jnp.where(row % g >= half, cum - mid, mid - cum))
            cross = lax.dot_general((q * factor).astype(BF16), (k * factor).astype(BF16), _NT,
                                    preferred_element_type=F32)
            pair = ((row // g) == (col // g)) & (row % g >= half) & (col % g < half)
            scores = jnp.where(pair, cross, scores)
        o = jnp.dot(scores.astype(BF16), vb, preferred_element_type=F32)

        st = st_ref[...]
        o = o + lax.dot_general((q * jnp.exp(cum)).astype(BF16), st.astype(BF16), _NT,
                                preferred_element_type=F32)
        k_end = (k * jnp.exp(cum_last - cum)).astype(BF16)
        st_ref[...] = st * jnp.exp(cum_last) + lax.dot_general(vb, k_end, _TN,
                                                                preferred_element_type=F32)
        ms = jnp.mean(o * o, axis=-1, keepdims=True)
        y = (o * lax.rsqrt(ms + EPS)) * g_ref[...]
        r = r_ref[pl.ds(r0, sub), :]
        xb_ref[pl.ds(r0, sub), :] = (y * _silu(r)).astype(BF16)
        return carry

    lax.fori_loop(0, seq_len // sub, sub_block, 0)
    sfin_ref[0, 0] = st_ref[...].T


def _gla_seq(zq, log_a, g_gla, *, n_seq, seq_len):
    h = GLA_HEADS
    blk = (seq_len, GLA_DK)

    def col(off):
        return pl.BlockSpec(blk, functools.partial(lambda b, hh, off: (b, off + hh), off=off))

    est = 2 * 5 * _nbytes(blk, F32) + 2 * _nbytes(blk, BF16) + 16 * GLA_SUB * GLA_SUB * 4
    return pl.pallas_call(
        functools.partial(_gla_seq_kernel, seq_len=seq_len),
        grid=(n_seq, h),
        in_specs=[col(0), col(h), col(2 * h), col(3 * h),
                  pl.BlockSpec(blk, lambda b, hh: (b, hh)),
                  pl.BlockSpec((1, GLA_DV), lambda b, hh: (0, 0))],
        out_specs=[pl.BlockSpec(blk, lambda b, hh: (b, hh)),
                   pl.BlockSpec((1, 1, GLA_DK, GLA_DV), lambda b, hh: (b, hh, 0, 0))],
        out_shape=[jax.ShapeDtypeStruct((n_seq * seq_len, GLA_VAL), BF16),
                   jax.ShapeDtypeStruct((n_seq, h, GLA_DK, GLA_DV), F32)],
        scratch_shapes=[pltpu.VMEM((GLA_DV, GLA_DK), F32),
                        pltpu.VMEM((GLA_SUB, GLA_DK), F32),
                        pltpu.VMEM((GLA_SUB, GLA_DK), F32)],
        compiler_params=pltpu.CompilerParams(
            dimension_semantics=("arbitrary", "arbitrary"),
            vmem_limit_bytes=_vmem_limit(est)),
        name="gla_seq",
    )(zq, zq, zq, zq, log_a, g_gla.reshape(1, GLA_DV))


def _to_column(x_row, eye):
    return jnp.sum(jnp.where(eye, x_row, 0.0), axis=1, keepdims=True)


def _gla_step_kernel(zq_ref, la_ref, g_ref, s0_ref, xb_ref, snew_ref):
    eye = (lax.broadcasted_iota(jnp.int32, (GLA_DK, GLA_DK), 0)
           == lax.broadcasted_iota(jnp.int32, (GLA_DK, GLA_DK), 1))
    pad_rows = 2 * SUBLANES
    for h in range(GLA_HEADS):
        def cols(base, width=GLA_DK, h=h):
            return slice(base + h * width, base + (h + 1) * width)
        q = zq_ref[0, :, cols(0)] * (GLA_DK ** -0.5)
        k = zq_ref[0, :, cols(GLA_KEY)]
        v = zq_ref[0, :, cols(2 * GLA_KEY)]
        r = zq_ref[0, :, cols(2 * GLA_KEY + GLA_VAL)]
        b = la_ref[0, :, cols(0)]
        qt = q * jnp.exp(b)
        kt = k * jnp.exp(-b)
        a = jnp.exp(b)
        s0 = s0_ref[0, h]
        score = jnp.sum(qt * kt, axis=-1, keepdims=True)
        o_inter = jnp.dot(jnp.broadcast_to(qt, (pad_rows, GLA_DK)).astype(BF16), s0.astype(BF16),
                          preferred_element_type=F32)[0:1, :]
        o = score * v + o_inter
        snew_ref[0, h] = s0 * _to_column(a, eye) + _to_column(k, eye) * v
        ms = jnp.mean(o * o, axis=-1, keepdims=True)
        y = (o * lax.rsqrt(ms + EPS)) * g_ref[...]
        xb_ref[0, :, cols(0, GLA_DV)] = (y * _silu(r)).astype(BF16)


def _gla_step(zq, log_a, g_gla, s0):
    n = zq.shape[0]
    wq = zq.shape[1]
    sblk = (1, GLA_HEADS, GLA_DK, GLA_DV)
    est = 4 * _nbytes(sblk, F32) + 8 * GLA_DK * GLA_DV * 4
    xb, s_new = pl.pallas_call(
        _gla_step_kernel,
        grid=(n,),
        in_specs=[pl.BlockSpec((1, 1, wq), lambda b: (b, 0, 0)),
                  pl.BlockSpec((1, 1, GLA_KEY), lambda b: (b, 0, 0)),
                  pl.BlockSpec((1, GLA_DV), lambda b: (0, 0)),
                  pl.BlockSpec(sblk, lambda b: (b, 0, 0, 0))],
        out_specs=[pl.BlockSpec((1, 1, GLA_VAL), lambda b: (b, 0, 0)),
                   pl.BlockSpec(sblk, lambda b: (b, 0, 0, 0))],
        out_shape=[jax.ShapeDtypeStruct((n, 1, GLA_VAL), BF16),
                   jax.ShapeDtypeStruct(s0.shape, F32)],
        compiler_params=pltpu.CompilerParams(
            dimension_semantics=("arbitrary",),
            vmem_limit_bytes=_vmem_limit(est)),
        name="gla_step",
    )(zq.reshape(n, 1, wq), log_a.reshape(n, 1, GLA_KEY), g_gla.reshape(1, GLA_DV), s0)
    return xb.reshape(n, GLA_VAL), s_new


_PLAN = dict(
    norm=512,
    short_conv=dict(tm=512, tn=256),
    qkvr=dict(tm=1024, tn=512),
    log_decay=512,
    merge=dict(tm=512, tn=256),
    mix_out=dict(tm=1024, tn=512),
    ffn_up=dict(tm=1024, tn=256),
    ffn_down=dict(tm=512, tn=512),
    ple=dict(tm=1024, tn=512),
)


def kernel(x_prompt, x_sample, p_prompt, p_sample, state_conv, state_gla, state_ffn, g_mix, w_in,
           w_alpha2, b_alpha, w_conv, w_out_conv, g_gla, w_out_gla, w_mix_out, g_ffn, w_up,
           w_ffn_conv, w_down, g_ple, w_ple_gate, w_ple_proj, g_final):
    assert g_mix.shape[0] == 1, "single-layer trunk"
    assert x_sample.shape[1] == 1, "the sample group advances one token per sequence"
    n_seq, seq_len, d = x_prompt.shape
    n_step = x_sample.shape[0]
    plan = _PLAN
    w_in_t = jnp.swapaxes(w_in[0], 0, 1)
    w_alr_t = w_in_t[OFF_ALR:OFF_GATE_A]
    w_down_b = w_down[0].astype(BF16)
    conv_state, gla_state, ffn_state = state_conv[0], state_gla[0], state_ffn[0]

    h0_q = x_prompt.reshape(n_seq * seq_len, d)
    h0_s = x_sample.reshape(n_step, d)
    p_q = p_prompt[0].reshape(n_seq * seq_len, -1).astype(BF16)
    p_s = p_sample[0].reshape(n_step, -1).astype(BF16)

    n1_q = _rmsnorm(h0_q, g_mix[0], BF16, plan["norm"])
    n1_s = _rmsnorm(h0_s, g_mix[0], BF16, plan["norm"])
    zq_q, zq_s = _qkvr_proj(n1_q, n1_s, w_in_t, **plan["qkvr"])
    la_q = _log_decay(n1_q, w_alr_t, w_alpha2[0], b_alpha[0], tm=plan["log_decay"])
    la_s = _log_decay(n1_s, w_alr_t, w_alpha2[0], b_alpha[0], tm=plan["log_decay"])
    (xa_q, conv_last), (xa_s, u_s) = _short_conv_branch(
        n1_q, n1_s, w_in_t, w_conv[0], conv_state.reshape(n_step, -1), seq_len=seq_len,
        **plan["short_conv"])
    xb_q, gla_q = _gla_seq(zq_q, la_q, g_gla[0], n_seq=n_seq, seq_len=seq_len)
    xb_s, gla_s = _gla_step(zq_s, la_s, g_gla[0], gla_state)

    mg_q, mg_s = _merge([n1_q, xa_q, xb_q], [n1_s, xa_s, xb_s], w_in_t, w_out_conv[0],
                        w_out_gla[0], **plan["merge"])
    h1_q, h1_s = _proj_residual("mix_out", mg_q, h0_q, mg_s, h0_s, w_mix_out[0],
                                **plan["mix_out"])

    n2_q = _rmsnorm(h1_q, g_ffn[0], BF16, plan["norm"])
    n2_s = _rmsnorm(h1_s, g_ffn[0], BF16, plan["norm"])
    (act_q, last_g, last_v), (act_s, ug_s, uv_s) = _conv_ffn_up(
        n2_q, n2_s, w_up[0], w_ffn_conv[0], ffn_state.reshape(n_step, -1), seq_len=seq_len,
        **plan["ffn_up"])
    h2_q, h2_s = _proj_residual("ffn_down", act_q, h1_q, act_s, h1_s, w_down_b,
                                **plan["ffn_down"])

    n3_q = _rmsnorm(h2_q, g_ple[0], BF16, plan["norm"])
    n3_s = _rmsnorm(h2_s, g_ple[0], BF16, plan["norm"])
    h3_q, h3_s = _ple([n3_q, p_q], h2_q, [n3_s, p_s], h2_s, w_ple_gate[0], w_ple_proj[0],
                      **plan["ple"])
    y_q = _rmsnorm(h3_q, g_final, F32, plan["norm"]).reshape(n_seq, seq_len, d)
    y_s = _rmsnorm(h3_s, g_final, F32, plan["norm"]).reshape(n_step, 1, d)

    keep = slice(SUBLANES - (CONV_W - 1), SUBLANES)
    conv_q = conv_last[:, keep, :]
    ffn_q = jnp.concatenate([last_g, last_v], axis=-1)[:, keep, :]
    conv_s = jnp.stack([conv_state[:, 1, :], u_s], axis=1)
    ffn_s = jnp.stack([ffn_state[:, 1, :], jnp.concatenate([ug_s, uv_s], axis=-1)], axis=1)
    return (y_q, y_s, conv_q[None], conv_s[None], gla_q[None], gla_s[None], ffn_q[None],
            ffn_s[None])
```

```python
import functools

import jax
import jax.numpy as jnp
from jax import lax
from jax.experimental import pallas as pl
from jax.experimental.pallas import tpu as pltpu

F32 = jnp.float32
BF16 = jnp.bfloat16

D_MODEL = 4096
D_CONV = 2048
CONV_W = 3
GLA_HEADS = 8
GLA_DK = 256
GLA_DV = 256
GLA_KEY = GLA_HEADS * GLA_DK
GLA_VAL = GLA_HEADS * GLA_DV
GATE_RANK = 16
GATE_NORM = 16.0
GLA_CHUNK = 32
D_FF = 11008
EPS = 1e-6

V7X_VMEM_BYTES = 64 * 2**20
V7X_VMEM_CAP = 60 * 2**20
SUBLANES = 8
LANES = 128
MXU_DIM = 256
GLA_SUB = MXU_DIM
GLA_STEP_SEQS = 4
CAST_CHUNK_ELEMS = 256 * 1024

OFF_HA = 0
OFF_CA = D_CONV
OFF_BA = 2 * D_CONV
OFF_QKVR = 3 * D_CONV
OFF_ALR = OFF_QKVR + 2 * GLA_KEY + 2 * GLA_VAL
OFF_GATE_A = OFF_ALR + GATE_RANK
OFF_GATE_B = OFF_GATE_A + D_MODEL

_NT = (((1,), (1,)), ((), ()))
_TN = (((0,), (0,)), ((), ()))


def _vmem_limit(nbytes):
    return int(min(V7X_VMEM_CAP, nbytes + 8 * 2**20))


def _nbytes(shape, dtype):
    n = 1
    for s in shape:
        n *= int(s)
    return n * jnp.dtype(dtype).itemsize


def _largest_divisor(n, limit, multiple):
    best = multiple
    for d in range(multiple, min(n, limit) + 1, multiple):
        if n % d == 0:
            best = d
    assert n % best == 0
    return best


def _rmsnorm_kernel(x_ref, g_ref, o_ref):
    x = x_ref[...]
    ms = jnp.mean(x * x, axis=-1, keepdims=True)
    o_ref[...] = ((x * lax.rsqrt(ms + EPS)) * g_ref[...]).astype(o_ref.dtype)


def _rmsnorm(x, g, out_dtype, tm):
    m, d = x.shape
    tm = min(m, tm)
    return pl.pallas_call(
        _rmsnorm_kernel,
        grid=(m // tm,),
        in_specs=[pl.BlockSpec((tm, d), lambda i: (i, 0)),
                  pl.BlockSpec((1, d), lambda i: (0, 0))],
        out_specs=pl.BlockSpec((tm, d), lambda i: (i, 0)),
        out_shape=jax.ShapeDtypeStruct((m, d), out_dtype),
        compiler_params=pltpu.CompilerParams(
            dimension_semantics=("arbitrary",),
            vmem_limit_bytes=_vmem_limit(6 * tm * d * 4)),
        name="rmsnorm",
    )(x, g.reshape(1, d))


def _cast_rows(src_ref, dst_ref):
    rows, cols = src_ref.shape
    ck = _largest_divisor(rows, max(2 * SUBLANES, CAST_CHUNK_ELEMS // cols), 2 * SUBLANES)

    def body(c, carry):
        r = pl.multiple_of(c * ck, ck)
        dst_ref[pl.ds(r, ck), :] = src_ref[pl.ds(r, ck), :].astype(BF16)
        return carry

    lax.fori_loop(0, rows // ck, body, 0)


def _fused_matmul(name, a_list, w_list, extra_list, out_list, epilogue, *, tm, tn, nj,
                  scratch=(), acc_history=False, zero_scratch=False, tail=None, flags=None):
    m = a_list[0].shape[0]
    assert m % tm == 0
    ni = m // tm
    t_a, t_extra, t_out, t_epilogue = tail if tail is not None else ([], [], [], None)
    assert len(t_a) in (0, len(a_list))
    na, nw, ne, no = len(a_list), len(w_list), len(extra_list), len(out_list)
    nta, nte, nto = len(t_a), len(t_extra), len(t_out)
    cast_ids = [n for n, w in enumerate(w_list) if w[0].dtype != BF16]

    def w_block(w, off, transposed):
        if transposed:
            assert off % SUBLANES == 0 and tn % SUBLANES == 0
            return ((pl.Element(tn), pl.Element(w.shape[1])),
                    (lambda j, i: (pl.multiple_of(off + j * tn, SUBLANES), 0)))
        assert off % tn == 0
        return (w.shape[0], tn), (lambda j, i: (0, off // tn + j))

    def w_tile_shape(w, transposed):
        return (tn, w.shape[1]) if transposed else (w.shape[0], tn)

    in_specs = [pl.BlockSpec((tm, a.shape[1]), lambda j, i: (i, 0)) for a in a_list]
    in_specs += [pl.BlockSpec(a.shape, lambda j, i: (0, 0)) for a in t_a]
    in_specs += [pl.BlockSpec(*w_block(w, off, tr)) for w, _, off, tr in w_list]
    in_specs += [pl.BlockSpec(blk, im) for _, blk, im in list(extra_list) + list(t_extra)]
    out_specs = [pl.BlockSpec(blk, im) for _, blk, im in list(out_list) + list(t_out)]
    out_shape = [s for s, _, _ in list(out_list) + list(t_out)]
    scratch_shapes = [pltpu.VMEM(w_tile_shape(w_list[n][0], w_list[n][3]), BF16) for n in cast_ids]
    n_acc = nw if acc_history else 0
    scratch_shapes += [pltpu.VMEM((SUBLANES + tm, tn), F32)] * n_acc
    scratch_shapes += list(scratch)

    def kernel(*refs):
        pos = 0

        def take(count):
            nonlocal pos
            pos += count
            return refs[pos - count:pos]

        a_refs, ta_refs, w_refs = take(na), take(nta), take(nw)
        e_refs, te_refs, o_refs, to_refs = take(ne), take(nte), take(no), take(nto)
        s_refs = refs[pos:]
        wb_refs = dict(zip(cast_ids, s_refs[:len(cast_ids)]))
        acc_refs = s_refs[len(cast_ids):len(cast_ids) + n_acc]
        user_scratch = s_refs[len(cast_ids) + n_acc:]
        j = pl.program_id(0)
        i = pl.program_id(1)

        @pl.when(i == 0)
        def _():
            for n in cast_ids:
                _cast_rows(w_refs[n], wb_refs[n])
            for ref in list(acc_refs) + (list(user_scratch) if zero_scratch else []):
                ref[...] = jnp.zeros_like(ref)

        def dots(lhs_refs, rows=slice(None)):
            accs = []
            for n, (_, ai, _, transposed) in enumerate(w_list):
                w_ref = wb_refs[n] if n in wb_refs else w_refs[n]
                a = lhs_refs[ai][rows, :]
                if transposed:
                    accs.append(lax.dot_general(a, w_ref[...], _NT, preferred_element_type=F32))
                else:
                    accs.append(jnp.dot(a, w_ref[...], preferred_element_type=F32))
            return accs

        accs = dots(a_refs)
        if acc_history:
            for n in range(nw):
                _push_history(acc_refs[n], accs[n], acc_refs[n])
            accs = list(acc_refs)
        epilogue(accs, e_refs, o_refs, user_scratch, j, i)

        if t_epilogue is not None:
            @pl.when(i == ni - 1)
            def _():
                t_epilogue(dots(ta_refs), te_refs, to_refs, j)

    est = 0
    for a in a_list:
        est += 2 * _nbytes((tm, a.shape[1]), a.dtype)
    for a in t_a:
        est += 2 * _nbytes(a.shape, a.dtype)
    for w, _, _, tr in w_list:
        est += 2 * _nbytes(w_tile_shape(w, tr), w.dtype)
        if w.dtype != BF16:
            est += _nbytes(w_tile_shape(w, tr), BF16)
    for arr, blk, _ in list(extra_list) + list(out_list) + list(t_extra) + list(t_out):
        est += 2 * _nbytes(blk, arr.dtype)
    est += 3 * nw * tm * tn * 4

    outs = pl.pallas_call(
        kernel,
        grid=(nj, ni),
        in_specs=in_specs,
        out_specs=out_specs,
        out_shape=out_shape,
        scratch_shapes=scratch_shapes,
        compiler_params=pltpu.CompilerParams(
            dimension_semantics=("arbitrary", "arbitrary"),
            vmem_limit_bytes=_vmem_limit(est), flags=flags),
        name=name,
    )(*a_list, *t_a, *[w[0] for w in w_list], *[e for e, _, _ in extra_list],
      *[e for e, _, _ in t_extra])
    return outs[:no], outs[no:]


def _tile(tm, tn):
    return (tm, tn), (lambda j, i: (i, j))


def _tail_tile(rows, tn, off_blocks=0):
    return (rows, tn), (lambda j, i: (0, off_blocks + j))


def _push_history(hist_ref, tile, prev_ref):
    prev_tail = _history_tail(prev_ref)
    hist_ref[SUBLANES:SUBLANES + tile.shape[0], :] = tile
    hist_ref[0:SUBLANES, :] = prev_tail


def _conv3_history(hist_ref, seq_start, w3):
    tm = hist_ref.shape[0] - SUBLANES
    if seq_start is not None:
        hist_ref[0:SUBLANES, :] = jnp.where(seq_start, 0.0, hist_ref[0:SUBLANES, :])
    taps = [hist_ref[SUBLANES - (CONV_W - 1) + t:SUBLANES - (CONV_W - 1) + t + tm, :]
            for t in range(CONV_W)]
    return w3[0:1, :] * taps[0] + w3[1:2, :] * taps[1] + w3[2:3, :] * taps[2]


def _history_tail(hist_ref):
    tm = hist_ref.shape[0] - SUBLANES
    return hist_ref[tm:tm + SUBLANES, :]


def _conv3_step(w3, prev2, prev1, cur):
    return w3[0:1, :] * prev2 + w3[1:2, :] * prev1 + w3[2:3, :] * cur


def _silu(x):
    return x * jax.nn.sigmoid(x)


def _short_conv_branch(n1_q, n1_s, w_in_t, w_conv, state2d, *, seq_len, tm, tn):
    m, ms = n1_q.shape[0], n1_s.shape[0]
    nj = D_CONV // tn
    tps = seq_len // tm
    n_seq = m // seq_len
    w_list = [(w_in_t, 0, OFF_HA, True), (w_in_t, 0, OFF_CA, True), (w_in_t, 0, OFF_BA, True)]
    wc = (w_conv, (CONV_W, tn), lambda j, i: (0, j))

    def epilogue(accs, e_refs, o_refs, s_refs, j, i):
        h, c, b = accs
        (wc_ref,), (xa_ref, last_ref), (u_hist,) = e_refs, o_refs, s_refs
        _push_history(u_hist, c * h, u_hist)
        cu = _conv3_history(u_hist, i % tps == 0, wc_ref[...])
        xa_ref[...] = (b * cu).astype(BF16)
        last_ref[0] = _history_tail(u_hist)

    def step_epilogue(accs, e_refs, o_refs, j):
        h, c, b = accs
        wc_ref, s0_ref, s1_ref = e_refs
        xa_ref, u_ref = o_refs
        u = c * h
        xa_ref[...] = (b * _conv3_step(wc_ref[...], s0_ref[...], s1_ref[...], u)).astype(BF16)
        u_ref[...] = u

    outs = [(jax.ShapeDtypeStruct((m, D_CONV), BF16),) + _tile(tm, tn),
            (jax.ShapeDtypeStruct((n_seq, SUBLANES, D_CONV), F32), (1, SUBLANES, tn),
             lambda j, i: (i // tps, 0, j))]
    tail = ([n1_s],
            [wc, (state2d,) + _tail_tile(ms, tn), (state2d,) + _tail_tile(ms, tn, nj)],
            [(jax.ShapeDtypeStruct((ms, D_CONV), BF16),) + _tail_tile(ms, tn),
             (jax.ShapeDtypeStruct((ms, D_CONV), F32),) + _tail_tile(ms, tn)],
            step_epilogue)
    return _fused_matmul("short_conv", [n1_q], w_list, [wc], outs, epilogue,
                         tm=tm, tn=tn, nj=nj, scratch=[pltpu.VMEM((SUBLANES + tm, tn), F32)],
                         zero_scratch=True, tail=tail)


def _qkvr_proj(n1_q, n1_s, w_in_t, *, tm, tn):
    m, ms = n1_q.shape[0], n1_s.shape[0]
    n = 2 * GLA_KEY + 2 * GLA_VAL

    def epilogue(accs, e_refs, o_refs, s_refs, j, i):
        o_refs[0][...] = accs[0]

    def step_epilogue(accs, e_refs, o_refs, j):
        o_refs[0][...] = accs[0]

    outs = [(jax.ShapeDtypeStruct((m, n), F32),) + _tile(tm, tn)]
    tail = ([n1_s], [], [(jax.ShapeDtypeStruct((ms, n), F32),) + _tail_tile(ms, tn)],
            step_epilogue)
    (zq,), (zq_s,) = _fused_matmul("qkvr_proj", [n1_q], [(w_in_t, 0, OFF_QKVR, True)], [], outs,
                                   epilogue, tm=tm, tn=tn, nj=n // tn, tail=tail)
    return zq, zq_s


def _log_decay_kernel(n_ref, w1_ref, w2_ref, b_ref, o_ref):
    a_lr = lax.dot_general(n_ref[...], w1_ref[...], _NT, preferred_element_type=F32)
    x = jnp.dot(a_lr.astype(BF16), w2_ref[...], preferred_element_type=F32) + b_ref[...]
    log_sig = jnp.minimum(x, 0.0) - jnp.log1p(jnp.exp(-jnp.abs(x)))
    o_ref[...] = log_sig / GATE_NORM


def _log_decay(n1, w_alr_t, w_alpha2, b_alpha, *, tm):
    m, d = n1.shape
    tm = min(m, tm)
    w1 = jnp.pad(w_alr_t, ((0, LANES - GATE_RANK), (0, 0))).astype(BF16)
    w2 = jnp.pad(w_alpha2, ((0, LANES - GATE_RANK), (0, 0))).astype(BF16)
    return pl.pallas_call(
        _log_decay_kernel,
        grid=(m // tm,),
        in_specs=[pl.BlockSpec((tm, d), lambda i: (i, 0)),
                  pl.BlockSpec((LANES, d), lambda i: (0, 0)),
                  pl.BlockSpec((LANES, GLA_KEY), lambda i: (0, 0)),
                  pl.BlockSpec((1, GLA_KEY), lambda i: (0, 0))],
        out_specs=pl.BlockSpec((tm, GLA_KEY), lambda i: (i, 0)),
        out_shape=jax.ShapeDtypeStruct((m, GLA_KEY), F32),
        compiler_params=pltpu.CompilerParams(
            dimension_semantics=("arbitrary",),
            vmem_limit_bytes=_vmem_limit(2 * tm * d * 2 + 6 * tm * GLA_KEY * 4 + 4 * 2**20)),
        name="log_decay",
    )(n1, w1, w2, b_alpha.reshape(1, GLA_KEY))


def _merge(lhs_q, lhs_s, w_in_t, w_out_conv, w_out_gla, *, tm, tn):
    m, ms = lhs_q[0].shape[0], lhs_s[0].shape[0]

    def merged(accs):
        ga, gb, ya, yb = accs
        return (jax.nn.sigmoid(ga) * ya + jax.nn.sigmoid(gb) * yb).astype(BF16)

    def epilogue(accs, e_refs, o_refs, s_refs, j, i):
        o_refs[0][...] = merged(accs)

    def step_epilogue(accs, e_refs, o_refs, j):
        o_refs[0][...] = merged(accs)

    w_list = [(w_in_t, 0, OFF_GATE_A, True), (w_in_t, 0, OFF_GATE_B, True),
              (w_out_conv, 1, 0, False), (w_out_gla, 2, 0, False)]
    outs = [(jax.ShapeDtypeStruct((m, D_MODEL), BF16),) + _tile(tm, tn)]
    tail = (list(lhs_s), [], [(jax.ShapeDtypeStruct((ms, D_MODEL), BF16),) + _tail_tile(ms, tn)],
            step_epilogue)
    (out,), (out_s,) = _fused_matmul("gated_merge", list(lhs_q), w_list, [], outs, epilogue,
                                     tm=tm, tn=tn, nj=D_MODEL // tn, tail=tail)
    return out, out_s


def _proj_residual(name, a_q, res_q, a_s, res_s, w, *, tm, tn):
    m, ms = a_q.shape[0], a_s.shape[0]
    n = w.shape[1]

    def epilogue(accs, e_refs, o_refs, s_refs, j, i):
        o_refs[0][...] = e_refs[0][...] + accs[0]

    def step_epilogue(accs, e_refs, o_refs, j):
        o_refs[0][...] = e_refs[0][...] + accs[0]

    extras = [(res_q,) + _tile(tm, tn)]
    outs = [(jax.ShapeDtypeStruct((m, n), F32),) + _tile(tm, tn)]
    tail = ([a_s], [(res_s,) + _tail_tile(ms, tn)],
            [(jax.ShapeDtypeStruct((ms, n), F32),) + _tail_tile(ms, tn)], step_epilogue)
    (out,), (out_s,) = _fused_matmul(name, [a_q], [(w, 0, 0, False)], extras, outs, epilogue,
                                     tm=tm, tn=tn, nj=n // tn, tail=tail)
    return out, out_s


def _conv_ffn_up(n2_q, n2_s, w_up, w_ffn_conv, state2d, *, seq_len, tm, tn):
    m, ms = n2_q.shape[0], n2_s.shape[0]
    nj = D_FF // tn
    tps = seq_len // tm
    n_seq = m // seq_len
    w_list = [(w_up, 0, 0, False), (w_up, 0, D_FF, False)]
    wcg = (w_ffn_conv, (CONV_W, tn), lambda j, i: (0, j))
    wcv = (w_ffn_conv, (CONV_W, tn), lambda j, i: (0, nj + j))

    def epilogue(accs, e_refs, o_refs, s_refs, j, i):
        ug_hist, uv_hist = accs
        wg_ref, wv_ref = e_refs
        act_ref, lastg_ref, lastv_ref = o_refs
        seq_start = i % tps == 0
        cg = _conv3_history(ug_hist, seq_start, wg_ref[...])
        cv = _conv3_history(uv_hist, seq_start, wv_ref[...])
        act_ref[...] = (_silu(cg) * cv).astype(BF16)
        lastg_ref[0] = _history_tail(ug_hist)
        lastv_ref[0] = _history_tail(uv_hist)

    def step_epilogue(accs, e_refs, o_refs, j):
        ug, uv = accs
        wg_ref, wv_ref, g0, v0, g1, v1 = e_refs
        act_ref, ug_ref, uv_ref = o_refs
        cg = _conv3_step(wg_ref[...], g0[...], g1[...], ug)
        cv = _conv3_step(wv_ref[...], v0[...], v1[...], uv)
        act_ref[...] = (_silu(cg) * cv).astype(BF16)
        ug_ref[...] = ug
        uv_ref[...] = uv

    last = (jax.ShapeDtypeStruct((n_seq, SUBLANES, D_FF), F32), (1, SUBLANES, tn),
            lambda j, i: (i // tps, 0, j))
    outs = [(jax.ShapeDtypeStruct((m, D_FF), BF16),) + _tile(tm, tn), last, last]
    up_s = (jax.ShapeDtypeStruct((ms, D_FF), F32),) + _tail_tile(ms, tn)
    tail = ([n2_s],
            [wcg, wcv] + [(state2d,) + _tail_tile(ms, tn, q * nj) for q in range(4)],
            [(jax.ShapeDtypeStruct((ms, D_FF), BF16),) + _tail_tile(ms, tn), up_s, up_s],
            step_epilogue)
    return _fused_matmul("conv_ffn_up", [n2_q], w_list, [wcg, wcv], outs, epilogue,
                         tm=tm, tn=tn, nj=nj, acc_history=True, tail=tail)


def _ple(lhs_q, h2_q, lhs_s, h2_s, w_ple_gate, w_ple_proj, *, tm, tn):
    m, ms = h2_q.shape[0], h2_s.shape[0]

    def epilogue(accs, e_refs, o_refs, s_refs, j, i):
        gate, proj = accs
        o_refs[0][...] = e_refs[0][...] + jax.nn.sigmoid(gate) * proj

    def step_epilogue(accs, e_refs, o_refs, j):
        gate, proj = accs
        o_refs[0][...] = e_refs[0][...] + jax.nn.sigmoid(gate) * proj

    extras = [(h2_q,) + _tile(tm, tn)]
    outs = [(jax.ShapeDtypeStruct((m, D_MODEL), F32),) + _tile(tm, tn)]
    tail = (list(lhs_s), [(h2_s,) + _tail_tile(ms, tn)],
            [(jax.ShapeDtypeStruct((ms, D_MODEL), F32),) + _tail_tile(ms, tn)], step_epilogue)
    (out,), (out_s,) = _fused_matmul(
        "ple_gate", list(lhs_q), [(w_ple_gate, 0, 0, False), (w_ple_proj, 1, 0, False)], extras,
        outs, epilogue, tm=tm, tn=tn, nj=D_MODEL // tn, tail=tail)
    return out, out_s


def _gla_seq_kernel(q_ref, k_ref, v_ref, r_ref, la_ref, g_ref, xb_ref, sfin_ref,
                    st_ref, b_sc, cum_sc, *, seq_len):
    sub, c = GLA_SUB, GLA_CHUNK
    n_chunks = sub // c
    st_ref[...] = jnp.zeros_like(st_ref)
    row = lax.broadcasted_iota(jnp.int32, (sub, sub), 0)
    col = lax.broadcasted_iota(jnp.int32, (sub, sub), 1)
    row_in_chunk = row % c
    groups = [g for g in (2 * c, 4 * c, 8 * c) if g <= sub]

    def sub_block(sb, carry):
        r0 = pl.multiple_of(sb * sub, sub)
        b = la_ref[pl.ds(r0, sub), :]
        s = 1
        while s < c:
            b = b + jnp.where(row_in_chunk >= s, pltpu.roll(b, s, axis=0), 0.0)
            s *= 2
        b_sc[...] = b
        offset = jnp.zeros((1, GLA_DK), F32)
        for n in range(n_chunks):
            cum_sc[n * c:(n + 1) * c, :] = b[n * c:(n + 1) * c, :] + offset
            offset = offset + b_sc[n * c + c - 1:n * c + c, :]
        cum = cum_sc[...]
        cum_last = offset

        q = q_ref[pl.ds(r0, sub), :] * (GLA_DK ** -0.5)
        k = k_ref[pl.ds(r0, sub), :]
        vb = v_ref[pl.ds(r0, sub), :].astype(BF16)

        qt = (q * jnp.exp(b)).astype(BF16)
        kt = (k * jnp.exp(-b)).astype(BF16)
        diag = ((row // c) == (col // c)) & (col <= row)
        scores = jnp.where(diag, lax.dot_general(qt, kt, _NT, preferred_element_type=F32), 0.0)
        for g in groups:
            half = g // 2
            mid = jnp.concatenate(
                [jnp.broadcast_to(cum_sc[grp * g + half - 1:grp * g + half, :], (g, GLA_DK))
                 for grp in range(sub // g)], axis=0)
            factor = jnp.exp(jnp.where(row % g >= half, cum - mid, mid - cum))
            cross = lax.dot_general((q * factor).astype(BF16), (k * factor).astype(BF16), _NT,
                                    preferred_element_type=F32)
            pair = ((row // g) == (col // g)) & (row % g >= half) & (col % g < half)
            scores = jnp.where(pair, cross, scores)
        o = jnp.dot(scores.astype(BF16), vb, preferred_element_type=F32)

        st = st_ref[...]
        o = o + lax.dot_general((q * jnp.exp(cum)).astype(BF16), st.astype(BF16), _NT,
                                preferred_element_type=F32)
        k_end = (k * jnp.exp(cum_last - cum)).astype(BF16)
        st_ref[...] = st * jnp.exp(cum_last) + lax.dot_general(vb, k_end, _TN,
                                                                preferred_element_type=F32)
        ms = jnp.mean(o * o, axis=-1, keepdims=True)
        y = (o * lax.rsqrt(ms + EPS)) * g_ref[...]
        r = r_ref[pl.ds(r0, sub), :]
        xb_ref[pl.ds(r0, sub), :] = (y * _silu(r)).astype(BF16)
        return carry

    lax.fori_loop(0, seq_len // sub, sub_block, 0)
    sfin_ref[0, 0] = st_ref[...].T


def _gla_seq(zq, log_a, g_gla, *, n_seq, seq_len):
    h = GLA_HEADS
    blk = (seq_len, GLA_DK)

    def col(off):
        return pl.BlockSpec(blk, functools.partial(lambda b, hh, off: (b, off + hh), off=off))

    est = 2 * 5 * _nbytes(blk, F32) + 2 * _nbytes(blk, BF16) + 16 * GLA_SUB * GLA_SUB * 4
    return pl.pallas_call(
        functools.partial(_gla_seq_kernel, seq_len=seq_len),
        grid=(n_seq, h),
        in_specs=[col(0), col(h), col(2 * h), col(3 * h),
                  pl.BlockSpec(blk, lambda b, hh: (b, hh)),
                  pl.BlockSpec((1, GLA_DV), lambda b, hh: (0, 0))],
        out_specs=[pl.BlockSpec(blk, lambda b, hh: (b, hh)),
                   pl.BlockSpec((1, 1, GLA_DK, GLA_DV), lambda b, hh: (b, hh, 0, 0))],
        out_shape=[jax.ShapeDtypeStruct((n_seq * seq_len, GLA_VAL), BF16),
                   jax.ShapeDtypeStruct((n_seq, h, GLA_DK, GLA_DV), F32)],
        scratch_shapes=[pltpu.VMEM((GLA_DV, GLA_DK), F32),
                        pltpu.VMEM((GLA_SUB, GLA_DK), F32),
                        pltpu.VMEM((GLA_SUB, GLA_DK), F32)],
        compiler_params=pltpu.CompilerParams(
            dimension_semantics=("arbitrary", "arbitrary"),
            vmem_limit_bytes=_vmem_limit(est)),
        name="gla_seq",
    )(zq, zq, zq, zq, log_a, g_gla.reshape(1, GLA_DV))


def _to_column(x_row, eye):
    return jnp.sum(jnp.where(eye, x_row, 0.0), axis=1, keepdims=True)


def _gla_step_kernel(zq_ref, la_ref, g_ref, s0_ref, xb_ref, snew_ref):
    eye = (lax.broadcasted_iota(jnp.int32, (GLA_DK, GLA_DK), 0)
           == lax.broadcasted_iota(jnp.int32, (GLA_DK, GLA_DK), 1))
    pad_rows = 2 * SUBLANES

    def one_sequence(s, carry):
        for h in range(GLA_HEADS):
            def cols(base, width=GLA_DK, h=h):
                return slice(base + h * width, base + (h + 1) * width)
            q = zq_ref[s, :, cols(0)] * (GLA_DK ** -0.5)
            k = zq_ref[s, :, cols(GLA_KEY)]
            v = zq_ref[s, :, cols(2 * GLA_KEY)]
            r = zq_ref[s, :, cols(2 * GLA_KEY + GLA_VAL)]
            b = la_ref[s, :, cols(0)]
            qt = q * jnp.exp(b)
            kt = k * jnp.exp(-b)
            a = jnp.exp(b)
            s0 = s0_ref[s, h]
            score = jnp.sum(qt * kt, axis=-1, keepdims=True)
            o_inter = jnp.dot(jnp.broadcast_to(qt, (pad_rows, GLA_DK)).astype(BF16),
                              s0.astype(BF16), preferred_element_type=F32)[0:1, :]
            o = score * v + o_inter
            snew_ref[s, h] = s0 * _to_column(a, eye) + _to_column(k, eye) * v
            ms = jnp.mean(o * o, axis=-1, keepdims=True)
            y = (o * lax.rsqrt(ms + EPS)) * g_ref[...]
            xb_ref[s, :, cols(0, GLA_DV)] = (y * _silu(r)).astype(BF16)
        return carry

    lax.fori_loop(0, zq_ref.shape[0], one_sequence, 0)


def _gla_step(zq, log_a, g_gla, s0):
    n = zq.shape[0]
    wq = zq.shape[1]
    per_step = GLA_STEP_SEQS if n % GLA_STEP_SEQS == 0 else 1
    sblk = (per_step, GLA_HEADS, GLA_DK, GLA_DV)
    est = 4 * _nbytes(sblk, F32) + 8 * GLA_DK * GLA_DV * 4
    xb, s_new = pl.pallas_call(
        _gla_step_kernel,
        grid=(n // per_step,),
        in_specs=[pl.BlockSpec((per_step, 1, wq), lambda b: (b, 0, 0)),
                  pl.BlockSpec((per_step, 1, GLA_KEY), lambda b: (b, 0, 0)),
                  pl.BlockSpec((1, GLA_DV), lambda b: (0, 0)),
                  pl.BlockSpec(sblk, lambda b: (b, 0, 0, 0))],
        out_specs=[pl.BlockSpec((per_step, 1, GLA_VAL), lambda b: (b, 0, 0)),
                   pl.BlockSpec(sblk, lambda b: (b, 0, 0, 0))],
        out_shape=[jax.ShapeDtypeStruct((n, 1, GLA_VAL), BF16),
                   jax.ShapeDtypeStruct(s0.shape, F32)],
        compiler_params=pltpu.CompilerParams(
            dimension_semantics=("arbitrary",),
            vmem_limit_bytes=_vmem_limit(est)),
        name="gla_step",
    )(zq.reshape(n, 1, wq), log_a.reshape(n, 1, GLA_KEY), g_gla.reshape(1, GLA_DV), s0)
    return xb.reshape(n, GLA_VAL), s_new


_PLAN = dict(
    norm=512,
    short_conv=dict(tm=1024, tn=256),
    qkvr=dict(tm=1024, tn=512),
    log_decay=512,
    merge=dict(tm=512, tn=256),
    mix_out=dict(tm=1024, tn=512),
    ffn_up=dict(tm=1024, tn=256),
    ffn_down=dict(tm=512, tn=512),
    ple=dict(tm=1024, tn=512),
)


def kernel(x_prompt, x_sample, p_prompt, p_sample, state_conv, state_gla, state_ffn, g_mix, w_in,
           w_alpha2, b_alpha, w_conv, w_out_conv, g_gla, w_out_gla, w_mix_out, g_ffn, w_up,
           w_ffn_conv, w_down, g_ple, w_ple_gate, w_ple_proj, g_final):
    assert g_mix.shape[0] == 1, "single-layer trunk"
    assert x_sample.shape[1] == 1, "the sample group advances one token per sequence"
    n_seq, seq_len, d = x_prompt.shape
    n_step = x_sample.shape[0]
    plan = _PLAN
    w_in_t = jnp.swapaxes(w_in[0], 0, 1)
    w_alr_t = w_in_t[OFF_ALR:OFF_GATE_A]
    w_down_b = w_down[0].astype(BF16)
    conv_state, gla_state, ffn_state = state_conv[0], state_gla[0], state_ffn[0]

    h0_q = x_prompt.reshape(n_seq * seq_len, d)
    h0_s = x_sample.reshape(n_step, d)
    p_q = p_prompt[0].reshape(n_seq * seq_len, -1).astype(BF16)
    p_s = p_sample[0].reshape(n_step, -1).astype(BF16)

    n1_q = _rmsnorm(h0_q, g_mix[0], BF16, plan["norm"])
    n1_s = _rmsnorm(h0_s, g_mix[0], BF16, plan["norm"])
    zq_q, zq_s = _qkvr_proj(n1_q, n1_s, w_in_t, **plan["qkvr"])
    la_q = _log_decay(n1_q, w_alr_t, w_alpha2[0], b_alpha[0], tm=plan["log_decay"])
    la_s = _log_decay(n1_s, w_alr_t, w_alpha2[0], b_alpha[0], tm=plan["log_decay"])
    (xa_q, conv_last), (xa_s, u_s) = _short_conv_branch(
        n1_q, n1_s, w_in_t, w_conv[0], conv_state.reshape(n_step, -1), seq_len=seq_len,
        **plan["short_conv"])
    xb_q, gla_q = _gla_seq(zq_q, la_q, g_gla[0], n_seq=n_seq, seq_len=seq_len)
    xb_s, gla_s = _gla_step(zq_s, la_s, g_gla[0], gla_state)

    mg_q, mg_s = _merge([n1_q, xa_q, xb_q], [n1_s, xa_s, xb_s], w_in_t, w_out_conv[0],
                        w_out_gla[0], **plan["merge"])
    h1_q, h1_s = _proj_residual("mix_out", mg_q, h0_q, mg_s, h0_s, w_mix_out[0],
                                **plan["mix_out"])

    n2_q = _rmsnorm(h1_q, g_ffn[0], BF16, plan["norm"])
    n2_s = _rmsnorm(h1_s, g_ffn[0], BF16, plan["norm"])
    (act_q, last_g, last_v), (act_s, ug_s, uv_s) = _conv_ffn_up(
        n2_q, n2_s, w_up[0], w_ffn_conv[0], ffn_state.reshape(n_step, -1), seq_len=seq_len,
        **plan["ffn_up"])
    h2_q, h2_s = _proj_residual("ffn_down", act_q, h1_q, act_s, h1_s, w_down_b,
                                **plan["ffn_down"])

    n3_q = _rmsnorm(h2_q, g_ple[0], BF16, plan["norm"])
    n3_s = _rmsnorm(h2_s, g_ple[0], BF16, plan["norm"])
    h3_q, h3_s = _ple([n3_q, p_q], h2_q, [n3_s, p_s], h2_s, w_ple_gate[0], w_ple_proj[0],
                      **plan["ple"])
    y_q = _rmsnorm(h3_q, g_final, F32, plan["norm"]).reshape(n_seq, seq_len, d)
    y_s = _rmsnorm(h3_s, g_final, F32, plan["norm"]).reshape(n_step, 1, d)

    keep = slice(SUBLANES - (CONV_W - 1), SUBLANES)
    conv_q = conv_last[:, keep, :]
    ffn_q = jnp.concatenate([last_g, last_v], axis=-1)[:, keep, :]
    conv_s = jnp.stack([conv_state[:, 1, :], u_s], axis=1)
    ffn_s = jnp.stack([ffn_state[:, 1, :], jnp.concatenate([ug_s, uv_s], axis=-1)], axis=1)
    return (y_q, y_s, conv_q[None], conv_s[None], gla_q[None], gla_s[None], ffn_q[None],
            ffn_s[None])
```

```python
import functools

import jax
import jax.numpy as jnp
from jax import lax
from jax.experimental import pallas as pl
from jax.experimental.pallas import tpu as pltpu

F32 = jnp.float32
BF16 = jnp.bfloat16

D_MODEL = 4096
D_CONV = 2048
CONV_W = 3
GLA_HEADS = 8
GLA_DK = 256
GLA_DV = 256
GLA_KEY = GLA_HEADS * GLA_DK
GLA_VAL = GLA_HEADS * GLA_DV
GATE_RANK = 16
GATE_NORM = 16.0
GLA_CHUNK = 32
D_FF = 11008
EPS = 1e-6

V7X_VMEM_BYTES = 64 * 2**20
V7X_VMEM_CAP = 60 * 2**20
SUBLANES = 8
LANES = 128
MXU_DIM = 256
GLA_SUB = MXU_DIM
GLA_STEP_SEQS = 4
CAST_CHUNK_ELEMS = 256 * 1024

OFF_HA = 0
OFF_CA = D_CONV
OFF_BA = 2 * D_CONV
OFF_QKVR = 3 * D_CONV
OFF_ALR = OFF_QKVR + 2 * GLA_KEY + 2 * GLA_VAL
OFF_GATE_A = OFF_ALR + GATE_RANK
OFF_GATE_B = OFF_GATE_A + D_MODEL

_NT = (((1,), (1,)), ((), ()))
_TN = (((0,), (0,)), ((), ()))


def _vmem_limit(nbytes):
    return int(min(V7X_VMEM_CAP, nbytes + 8 * 2**20))


def _nbytes(shape, dtype):
    n = 1
    for s in shape:
        n *= int(s)
    return n * jnp.dtype(dtype).itemsize


def _largest_divisor(n, limit, multiple):
    best = multiple
    for d in range(multiple, min(n, limit) + 1, multiple):
        if n % d == 0:
            best = d
    assert n % best == 0
    return best


def _rmsnorm_kernel(x_ref, g_ref, o_ref):
    x = x_ref[...]
    ms = jnp.mean(x * x, axis=-1, keepdims=True)
    o_ref[...] = ((x * lax.rsqrt(ms + EPS)) * g_ref[...]).astype(o_ref.dtype)


def _rmsnorm(x, g, out_dtype, tm):
    m, d = x.shape
    tm = min(m, tm)
    return pl.pallas_call(
        _rmsnorm_kernel,
        grid=(m // tm,),
        in_specs=[pl.BlockSpec((tm, d), lambda i: (i, 0)),
                  pl.BlockSpec((1, d), lambda i: (0, 0))],
        out_specs=pl.BlockSpec((tm, d), lambda i: (i, 0)),
        out_shape=jax.ShapeDtypeStruct((m, d), out_dtype),
        compiler_params=pltpu.CompilerParams(
            dimension_semantics=("arbitrary",),
            vmem_limit_bytes=_vmem_limit(6 * tm * d * 4)),
        name="rmsnorm",
    )(x, g.reshape(1, d))


def _cast_rows(src_ref, dst_ref):
    rows, cols = src_ref.shape
    ck = _largest_divisor(rows, max(2 * SUBLANES, CAST_CHUNK_ELEMS // cols), 2 * SUBLANES)

    def body(c, carry):
        r = pl.multiple_of(c * ck, ck)
        dst_ref[pl.ds(r, ck), :] = src_ref[pl.ds(r, ck), :].astype(BF16)
        return carry

    lax.fori_loop(0, rows // ck, body, 0)


def _fused_matmul(name, a_list, w_list, extra_list, out_list, epilogue, *, tm, tn, nj,
                  scratch=(), acc_history=False, zero_scratch=False, tail=None):
    m = a_list[0].shape[0]
    assert m % tm == 0
    ni = m // tm
    t_a, t_extra, t_out, t_epilogue = tail if tail is not None else ([], [], [], None)
    assert len(t_a) in (0, len(a_list))
    na, nw, ne, no = len(a_list), len(w_list), len(extra_list), len(out_list)
    nta, nte, nto = len(t_a), len(t_extra), len(t_out)
    cast_ids = [n for n, w in enumerate(w_list) if w[0].dtype != BF16]

    def w_block(w, off, transposed):
        if transposed:
            assert off % SUBLANES == 0 and tn % SUBLANES == 0
            return ((pl.Element(tn), pl.Element(w.shape[1])),
                    (lambda j, i: (pl.multiple_of(off + j * tn, SUBLANES), 0)))
        assert off % tn == 0
        return (w.shape[0], tn), (lambda j, i: (0, off // tn + j))

    def w_tile_shape(w, transposed):
        return (tn, w.shape[1]) if transposed else (w.shape[0], tn)

    in_specs = [pl.BlockSpec((tm, a.shape[1]), lambda j, i: (i, 0)) for a in a_list]
    in_specs += [pl.BlockSpec(a.shape, lambda j, i: (0, 0)) for a in t_a]
    in_specs += [pl.BlockSpec(*w_block(w, off, tr)) for w, _, off, tr in w_list]
    in_specs += [pl.BlockSpec(blk, im) for _, blk, im in list(extra_list) + list(t_extra)]
    out_specs = [pl.BlockSpec(blk, im) for _, blk, im in list(out_list) + list(t_out)]
    out_shape = [s for s, _, _ in list(out_list) + list(t_out)]
    scratch_shapes = [pltpu.VMEM(w_tile_shape(w_list[n][0], w_list[n][3]), BF16) for n in cast_ids]
    n_acc = nw if acc_history else 0
    scratch_shapes += [pltpu.VMEM((2 * SUBLANES + tm, tn), F32)] * n_acc
    scratch_shapes += list(scratch)

    def kernel(*refs):
        pos = 0

        def take(count):
            nonlocal pos
            pos += count
            return refs[pos - count:pos]

        a_refs, ta_refs, w_refs = take(na), take(nta), take(nw)
        e_refs, te_refs, o_refs, to_refs = take(ne), take(nte), take(no), take(nto)
        s_refs = refs[pos:]
        wb_refs = dict(zip(cast_ids, s_refs[:len(cast_ids)]))
        acc_refs = s_refs[len(cast_ids):len(cast_ids) + n_acc]
        user_scratch = s_refs[len(cast_ids) + n_acc:]
        j = pl.program_id(0)
        i = pl.program_id(1)

        @pl.when(i == 0)
        def _():
            for n in cast_ids:
                _cast_rows(w_refs[n], wb_refs[n])
            for ref in list(acc_refs) + (list(user_scratch) if zero_scratch else []):
                ref[...] = jnp.zeros_like(ref)

        def dots(lhs_refs):
            accs = []
            for n, (_, ai, _, transposed) in enumerate(w_list):
                w_ref = wb_refs[n] if n in wb_refs else w_refs[n]
                a = lhs_refs[ai][...]
                if transposed:
                    accs.append(lax.dot_general(a, w_ref[...], _NT, preferred_element_type=F32))
                else:
                    accs.append(jnp.dot(a, w_ref[...], preferred_element_type=F32))
            return accs

        accs = dots(a_refs)
        if acc_history:
            for n in range(nw):
                _push_interleaved(acc_refs[n], accs[n])
            accs = list(acc_refs)
        epilogue(accs, e_refs, o_refs, user_scratch, j, i)

        if t_epilogue is not None:
            @pl.when(i == ni - 1)
            def _():
                t_epilogue(dots(ta_refs), te_refs, to_refs, j)

    est = 0
    for a in a_list:
        est += 2 * _nbytes((tm, a.shape[1]), a.dtype)
    for a in t_a:
        est += 2 * _nbytes(a.shape, a.dtype)
    for w, _, _, tr in w_list:
        est += 2 * _nbytes(w_tile_shape(w, tr), w.dtype)
        if w.dtype != BF16:
            est += _nbytes(w_tile_shape(w, tr), BF16)
    for arr, blk, _ in list(extra_list) + list(out_list) + list(t_extra) + list(t_out):
        est += 2 * _nbytes(blk, arr.dtype)
    est += 3 * nw * tm * tn * 4

    outs = pl.pallas_call(
        kernel,
        grid=(nj, ni),
        in_specs=in_specs,
        out_specs=out_specs,
        out_shape=out_shape,
        scratch_shapes=scratch_shapes,
        compiler_params=pltpu.CompilerParams(
            dimension_semantics=("arbitrary", "arbitrary"),
            vmem_limit_bytes=_vmem_limit(est)),
        name=name,
    )(*a_list, *t_a, *[w[0] for w in w_list], *[e for e, _, _ in extra_list],
      *[e for e, _, _ in t_extra])
    return outs[:no], outs[no:]


def _tile(tm, tn):
    return (tm, tn), (lambda j, i: (i, j))


def _tail_tile(rows, tn, off_blocks=0):
    return (rows, tn), (lambda j, i: (0, off_blocks + j))


def _push_history(hist_ref, tile, prev_ref):
    prev_tail = _history_tail(prev_ref)
    hist_ref[SUBLANES:SUBLANES + tile.shape[0], :] = tile
    hist_ref[0:SUBLANES, :] = prev_tail


def _conv3_history(hist_ref, seq_start, w3):
    tm = hist_ref.shape[0] - SUBLANES
    if seq_start is not None:
        hist_ref[0:SUBLANES, :] = jnp.where(seq_start, 0.0, hist_ref[0:SUBLANES, :])
    taps = [hist_ref[SUBLANES - (CONV_W - 1) + t:SUBLANES - (CONV_W - 1) + t + tm, :]
            for t in range(CONV_W)]
    return w3[0:1, :] * taps[0] + w3[1:2, :] * taps[1] + w3[2:3, :] * taps[2]


def _history_tail(hist_ref):
    tm = hist_ref.shape[0] - SUBLANES
    return hist_ref[tm:tm + SUBLANES, :]


def _stage_shape(tm, tn):
    return (tn // LANES, tm, LANES)


def _store_interleaved(o_ref, stage_ref, natural):
    nv = natural.shape[0] // SUBLANES
    for c in range(stage_ref.shape[0]):
        lanes = slice(c * LANES, (c + 1) * LANES)
        for s in range(SUBLANES):
            stage_ref[c, pl.ds(s, nv, stride=SUBLANES), :] = natural[s * nv:(s + 1) * nv, lanes]
        o_ref[:, lanes] = stage_ref[c]


def _store_natural(o_ref, stage_ref, interleaved):
    nv = interleaved.shape[0] // SUBLANES
    for c in range(stage_ref.shape[0]):
        lanes = slice(c * LANES, (c + 1) * LANES)
        stage_ref[c] = interleaved[:, lanes]
        for s in range(SUBLANES):
            o_ref[s * nv:(s + 1) * nv, lanes] = stage_ref[c, pl.ds(s, nv, stride=SUBLANES), :]


def _interleave_rows(x, group):
    m, n = x.shape
    nv = group // SUBLANES
    return x.reshape(m // group, SUBLANES, nv, n).swapaxes(1, 2).reshape(m, n)


def _push_interleaved(hist_ref, tile):
    tm = tile.shape[0]
    two = 2 * SUBLANES
    prev_last = hist_ref[tm:tm + two, :]
    hist_ref[two:two + tm, :] = tile
    row = lax.broadcasted_iota(jnp.int32, (SUBLANES, tile.shape[1]), 0)
    for k in range(2):
        cur = tile[tm - two + SUBLANES * k:tm - SUBLANES + SUBLANES * k, :]
        carry = prev_last[SUBLANES * k + SUBLANES - 1:SUBLANES * (k + 1), :]
        hist_ref[SUBLANES * k:SUBLANES * (k + 1), :] = jnp.where(
            row == 0, carry, pltpu.roll(cur, 1, axis=0))


def _conv3_interleaved(hist_ref, seq_start, w3):
    two = 2 * SUBLANES
    tm = hist_ref.shape[0] - two
    row = lax.broadcasted_iota(jnp.int32, (two, hist_ref.shape[1]), 0)
    hist_ref[0:two, :] = jnp.where(seq_start & (row % SUBLANES == 0), 0.0, hist_ref[0:two, :])
    taps = [hist_ref[SUBLANES * t:SUBLANES * t + tm, :] for t in range(CONV_W)]
    return w3[0:1, :] * taps[0] + w3[1:2, :] * taps[1] + w3[2:3, :] * taps[2]


def _last_two_tokens(hist_ref):
    tm = hist_ref.shape[0] - 2 * SUBLANES
    u_a = hist_ref[tm:tm + SUBLANES, :]
    u_b = hist_ref[tm + SUBLANES:tm + 2 * SUBLANES, :]
    row = lax.broadcasted_iota(jnp.int32, u_a.shape, 0)
    return jnp.where(row == SUBLANES - 2, pltpu.roll(u_a, SUBLANES - 1, axis=0), u_b)


def _conv3_step(w3, prev2, prev1, cur):
    return w3[0:1, :] * prev2 + w3[1:2, :] * prev1 + w3[2:3, :] * cur


def _silu(x):
    return x * jax.nn.sigmoid(x)


def _short_conv_branch(n1_q, n1_s, w_in_t, w_conv, state2d, *, seq_len, tm, tn):
    m, ms = n1_q.shape[0], n1_s.shape[0]
    nj = D_CONV // tn
    tps = seq_len // tm
    n_seq = m // seq_len
    w_list = [(w_in_t, 0, OFF_HA, True), (w_in_t, 0, OFF_CA, True), (w_in_t, 0, OFF_BA, True)]
    wc = (w_conv, (CONV_W, tn), lambda j, i: (0, j))

    def epilogue(accs, e_refs, o_refs, s_refs, j, i):
        h, c, b = accs
        (wc_ref,), (xa_ref, last_ref), (u_hist,) = e_refs, o_refs, s_refs
        _push_history(u_hist, c * h, u_hist)
        cu = _conv3_history(u_hist, i % tps == 0, wc_ref[...])
        xa_ref[...] = (b * cu).astype(BF16)
        last_ref[0] = _history_tail(u_hist)

    def step_epilogue(accs, e_refs, o_refs, j):
        h, c, b = accs
        wc_ref, s0_ref, s1_ref = e_refs
        xa_ref, u_ref = o_refs
        u = c * h
        xa_ref[...] = (b * _conv3_step(wc_ref[...], s0_ref[...], s1_ref[...], u)).astype(BF16)
        u_ref[...] = u

    outs = [(jax.ShapeDtypeStruct((m, D_CONV), BF16),) + _tile(tm, tn),
            (jax.ShapeDtypeStruct((n_seq, SUBLANES, D_CONV), F32), (1, SUBLANES, tn),
             lambda j, i: (i // tps, 0, j))]
    tail = ([n1_s],
            [wc, (state2d,) + _tail_tile(ms, tn), (state2d,) + _tail_tile(ms, tn, nj)],
            [(jax.ShapeDtypeStruct((ms, D_CONV), BF16),) + _tail_tile(ms, tn),
             (jax.ShapeDtypeStruct((ms, D_CONV), F32),) + _tail_tile(ms, tn)],
            step_epilogue)
    return _fused_matmul("short_conv", [n1_q], w_list, [wc], outs, epilogue,
                         tm=tm, tn=tn, nj=nj, scratch=[pltpu.VMEM((SUBLANES + tm, tn), F32)],
                         zero_scratch=True, tail=tail)


def _qkvr_proj(n1_q, n1_s, w_in_t, *, tm, tn):
    m, ms = n1_q.shape[0], n1_s.shape[0]
    n = 2 * GLA_KEY + 2 * GLA_VAL

    def epilogue(accs, e_refs, o_refs, s_refs, j, i):
        o_refs[0][...] = accs[0]

    def step_epilogue(accs, e_refs, o_refs, j):
        o_refs[0][...] = accs[0]

    outs = [(jax.ShapeDtypeStruct((m, n), F32),) + _tile(tm, tn)]
    tail = ([n1_s], [], [(jax.ShapeDtypeStruct((ms, n), F32),) + _tail_tile(ms, tn)],
            step_epilogue)
    (zq,), (zq_s,) = _fused_matmul("qkvr_proj", [n1_q], [(w_in_t, 0, OFF_QKVR, True)], [], outs,
                                   epilogue, tm=tm, tn=tn, nj=n // tn, tail=tail)
    return zq, zq_s


def _log_decay_kernel(n_ref, w1_ref, w2_ref, b_ref, o_ref):
    a_lr = lax.dot_general(n_ref[...], w1_ref[...], _NT, preferred_element_type=F32)
    x = jnp.dot(a_lr.astype(BF16), w2_ref[...], preferred_element_type=F32) + b_ref[...]
    log_sig = jnp.minimum(x, 0.0) - jnp.log1p(jnp.exp(-jnp.abs(x)))
    o_ref[...] = log_sig / GATE_NORM


def _log_decay(n1, w_alr_t, w_alpha2, b_alpha, *, tm):
    m, d = n1.shape
    tm = min(m, tm)
    w1 = jnp.pad(w_alr_t, ((0, LANES - GATE_RANK), (0, 0))).astype(BF16)
    w2 = jnp.pad(w_alpha2, ((0, LANES - GATE_RANK), (0, 0))).astype(BF16)
    return pl.pallas_call(
        _log_decay_kernel,
        grid=(m // tm,),
        in_specs=[pl.BlockSpec((tm, d), lambda i: (i, 0)),
                  pl.BlockSpec((LANES, d), lambda i: (0, 0)),
                  pl.BlockSpec((LANES, GLA_KEY), lambda i: (0, 0)),
                  pl.BlockSpec((1, GLA_KEY), lambda i: (0, 0))],
        out_specs=pl.BlockSpec((tm, GLA_KEY), lambda i: (i, 0)),
        out_shape=jax.ShapeDtypeStruct((m, GLA_KEY), F32),
        compiler_params=pltpu.CompilerParams(
            dimension_semantics=("arbitrary",),
            vmem_limit_bytes=_vmem_limit(2 * tm * d * 2 + 6 * tm * GLA_KEY * 4 + 4 * 2**20)),
        name="log_decay",
    )(n1, w1, w2, b_alpha.reshape(1, GLA_KEY))


def _merge(lhs_q, lhs_s, w_in_t, w_out_conv, w_out_gla, *, tm, tn):
    m, ms = lhs_q[0].shape[0], lhs_s[0].shape[0]

    def merged(accs):
        ga, gb, ya, yb = accs
        return (jax.nn.sigmoid(ga) * ya + jax.nn.sigmoid(gb) * yb).astype(BF16)

    def epilogue(accs, e_refs, o_refs, s_refs, j, i):
        o_refs[0][...] = merged(accs)

    def step_epilogue(accs, e_refs, o_refs, j):
        o_refs[0][...] = merged(accs)

    w_list = [(w_in_t, 0, OFF_GATE_A, True), (w_in_t, 0, OFF_GATE_B, True),
              (w_out_conv, 1, 0, False), (w_out_gla, 2, 0, False)]
    outs = [(jax.ShapeDtypeStruct((m, D_MODEL), BF16),) + _tile(tm, tn)]
    tail = (list(lhs_s), [], [(jax.ShapeDtypeStruct((ms, D_MODEL), BF16),) + _tail_tile(ms, tn)],
            step_epilogue)
    (out,), (out_s,) = _fused_matmul("gated_merge", list(lhs_q), w_list, [], outs, epilogue,
                                     tm=tm, tn=tn, nj=D_MODEL // tn, tail=tail)
    return out, out_s


def _proj_residual(name, a_q, res_q, a_s, res_s, w, *, tm, tn, interleave_out=False):
    m, ms = a_q.shape[0], a_s.shape[0]
    n = w.shape[1]

    def epilogue(accs, e_refs, o_refs, s_refs, j, i):
        if interleave_out:
            _store_interleaved(o_refs[0], s_refs[0], e_refs[0][...] + accs[0])
        else:
            o_refs[0][...] = e_refs[0][...] + accs[0]

    def step_epilogue(accs, e_refs, o_refs, j):
        o_refs[0][...] = e_refs[0][...] + accs[0]

    extras = [(res_q,) + _tile(tm, tn)]
    outs = [(jax.ShapeDtypeStruct((m, n), F32),) + _tile(tm, tn)]
    tail = ([a_s], [(res_s,) + _tail_tile(ms, tn)],
            [(jax.ShapeDtypeStruct((ms, n), F32),) + _tail_tile(ms, tn)], step_epilogue)
    scratch = [pltpu.VMEM(_stage_shape(tm, tn), F32)] if interleave_out else []
    (out,), (out_s,) = _fused_matmul(name, [a_q], [(w, 0, 0, False)], extras, outs, epilogue,
                                     tm=tm, tn=tn, nj=n // tn, tail=tail, scratch=scratch)
    return out, out_s


def _conv_ffn_up(n2_q, n2_s, w_up, w_ffn_conv, state2d, *, seq_len, tm, tn):
    m, ms = n2_q.shape[0], n2_s.shape[0]
    nj = D_FF // tn
    tps = seq_len // tm
    n_seq = m // seq_len
    w_list = [(w_up, 0, 0, False), (w_up, 0, D_FF, False)]
    wcg = (w_ffn_conv, (CONV_W, tn), lambda j, i: (0, j))
    wcv = (w_ffn_conv, (CONV_W, tn), lambda j, i: (0, nj + j))

    def epilogue(accs, e_refs, o_refs, s_refs, j, i):
        ug_hist, uv_hist = accs
        wg_ref, wv_ref = e_refs
        act_ref, lastg_ref, lastv_ref = o_refs
        seq_start = i % tps == 0
        cg = _conv3_interleaved(ug_hist, seq_start, wg_ref[...])
        cv = _conv3_interleaved(uv_hist, seq_start, wv_ref[...])
        act_ref[...] = (_silu(cg) * cv).astype(BF16)
        lastg_ref[0] = _last_two_tokens(ug_hist)
        lastv_ref[0] = _last_two_tokens(uv_hist)

    def step_epilogue(accs, e_refs, o_refs, j):
        ug, uv = accs
        wg_ref, wv_ref, g0, v0, g1, v1 = e_refs
        act_ref, ug_ref, uv_ref = o_refs
        cg = _conv3_step(wg_ref[...], g0[...], g1[...], ug)
        cv = _conv3_step(wv_ref[...], v0[...], v1[...], uv)
        act_ref[...] = (_silu(cg) * cv).astype(BF16)
        ug_ref[...] = ug
        uv_ref[...] = uv

    last = (jax.ShapeDtypeStruct((n_seq, SUBLANES, D_FF), F32), (1, SUBLANES, tn),
            lambda j, i: (i // tps, 0, j))
    outs = [(jax.ShapeDtypeStruct((m, D_FF), BF16),) + _tile(tm, tn), last, last]
    up_s = (jax.ShapeDtypeStruct((ms, D_FF), F32),) + _tail_tile(ms, tn)
    tail = ([n2_s],
            [wcg, wcv] + [(state2d,) + _tail_tile(ms, tn, q * nj) for q in range(4)],
            [(jax.ShapeDtypeStruct((ms, D_FF), BF16),) + _tail_tile(ms, tn), up_s, up_s],
            step_epilogue)
    return _fused_matmul("conv_ffn_up", [n2_q], w_list, [wcg, wcv], outs, epilogue,
                         tm=tm, tn=tn, nj=nj, acc_history=True, tail=tail)


def _ple(lhs_q, h2_q, lhs_s, h2_s, w_ple_gate, w_ple_proj, *, tm, tn):
    m, ms = h2_q.shape[0], h2_s.shape[0]

    def epilogue(accs, e_refs, o_refs, s_refs, j, i):
        gate, proj = accs
        _store_natural(o_refs[0], s_refs[0], e_refs[0][...] + jax.nn.sigmoid(gate) * proj)

    def step_epilogue(accs, e_refs, o_refs, j):
        gate, proj = accs
        o_refs[0][...] = e_refs[0][...] + jax.nn.sigmoid(gate) * proj

    extras = [(h2_q,) + _tile(tm, tn)]
    outs = [(jax.ShapeDtypeStruct((m, D_MODEL), F32),) + _tile(tm, tn)]
    tail = (list(lhs_s), [(h2_s,) + _tail_tile(ms, tn)],
            [(jax.ShapeDtypeStruct((ms, D_MODEL), F32),) + _tail_tile(ms, tn)], step_epilogue)
    (out,), (out_s,) = _fused_matmul(
        "ple_gate", list(lhs_q), [(w_ple_gate, 0, 0, False), (w_ple_proj, 1, 0, False)], extras,
        outs, epilogue, tm=tm, tn=tn, nj=D_MODEL // tn, tail=tail,
        scratch=[pltpu.VMEM(_stage_shape(tm, tn), F32)])
    return out, out_s


def _gla_seq_kernel(q_ref, k_ref, v_ref, r_ref, la_ref, g_ref, xb_ref, sfin_ref,
                    st_ref, b_sc, cum_sc, *, seq_len):
    sub, c = GLA_SUB, GLA_CHUNK
    n_chunks = sub // c
    st_ref[...] = jnp.zeros_like(st_ref)
    row = lax.broadcasted_iota(jnp.int32, (sub, sub), 0)
    col = lax.broadcasted_iota(jnp.int32, (sub, sub), 1)
    row_in_chunk = row % c
    groups = [g for g in (2 * c, 4 * c, 8 * c) if g <= sub]

    def sub_block(sb, carry):
        r0 = pl.multiple_of(sb * sub, sub)
        b = la_ref[pl.ds(r0, sub), :]
        s = 1
        while s < c:
            b = b + jnp.where(row_in_chunk >= s, pltpu.roll(b, s, axis=0), 0.0)
            s *= 2
        b_sc[...] = b
        offset = jnp.zeros((1, GLA_DK), F32)
        for n in range(n_chunks):
            cum_sc[n * c:(n + 1) * c, :] = b[n * c:(n + 1) * c, :] + offset
            offset = offset + b_sc[n * c + c - 1:n * c + c, :]
        cum = cum_sc[...]
        cum_last = offset

        q = q_ref[pl.ds(r0, sub), :] * (GLA_DK ** -0.5)
        k = k_ref[pl.ds(r0, sub), :]
        vb = v_ref[pl.ds(r0, sub), :].astype(BF16)

        qt = (q * jnp.exp(b)).astype(BF16)
        kt = (k * jnp.exp(-b)).astype(BF16)
        diag = ((row // c) == (col // c)) & (col <= row)
        scores = jnp.where(diag, lax.dot_general(qt, kt, _NT, preferred_element_type=F32), 0.0)
        for g in groups:
            half = g // 2
            mid = jnp.concatenate(
                [jnp.broadcast_to(cum_sc[grp * g + half - 1:grp * g + half, :], (g, GLA_DK))
                 for grp in range(sub // g)], axis=0)
            factor = jnp.exp(jnp.where(row % g >= half, cum - mid, mid - cum))
            cross = lax.dot_general((q * factor).astype(BF16), (k * factor).astype(BF16), _NT,
                                    preferred_element_type=F32)
            pair = ((row // g) == (col // g)) & (row % g >= half) & (col % g < half)
            scores = jnp.where(pair, cross, scores)
        o = jnp.dot(scores.astype(BF16), vb, preferred_element_type=F32)

        st = st_ref[...]
        o = o + lax.dot_general((q * jnp.exp(cum)).astype(BF16), st.astype(BF16), _NT,
                                preferred_element_type=F32)
        k_end = (k * jnp.exp(cum_last - cum)).astype(BF16)
        st_ref[...] = st * jnp.exp(cum_last) + lax.dot_general(vb, k_end, _TN,
                                                                preferred_element_type=F32)
        ms = jnp.mean(o * o, axis=-1, keepdims=True)
        y = (o * lax.rsqrt(ms + EPS)) * g_ref[...]
        r = r_ref[pl.ds(r0, sub), :]
        xb_ref[pl.ds(r0, sub), :] = (y * _silu(r)).astype(BF16)
        return carry

    lax.fori_loop(0, seq_len // sub, sub_block, 0)
    sfin_ref[0, 0] = st_ref[...].T


def _gla_seq(zq, log_a, g_gla, *, n_seq, seq_len):
    h = GLA_HEADS
    blk = (seq_len, GLA_DK)

    def col(off):
        return pl.BlockSpec(blk, functools.partial(lambda b, hh, off: (b, off + hh), off=off))

    est = 2 * 5 * _nbytes(blk, F32) + 2 * _nbytes(blk, BF16) + 16 * GLA_SUB * GLA_SUB * 4
    return pl.pallas_call(
        functools.partial(_gla_seq_kernel, seq_len=seq_len),
        grid=(n_seq, h),
        in_specs=[col(0), col(h), col(2 * h), col(3 * h),
                  pl.BlockSpec(blk, lambda b, hh: (b, hh)),
                  pl.BlockSpec((1, GLA_DV), lambda b, hh: (0, 0))],
        out_specs=[pl.BlockSpec(blk, lambda b, hh: (b, hh)),
                   pl.BlockSpec((1, 1, GLA_DK, GLA_DV), lambda b, hh: (b, hh, 0, 0))],
        out_shape=[jax.ShapeDtypeStruct((n_seq * seq_len, GLA_VAL), BF16),
                   jax.ShapeDtypeStruct((n_seq, h, GLA_DK, GLA_DV), F32)],
        scratch_shapes=[pltpu.VMEM((GLA_DV, GLA_DK), F32),
                        pltpu.VMEM((GLA_SUB, GLA_DK), F32),
                        pltpu.VMEM((GLA_SUB, GLA_DK), F32)],
        compiler_params=pltpu.CompilerParams(
            dimension_semantics=("arbitrary", "arbitrary"),
            vmem_limit_bytes=_vmem_limit(est)),
        name="gla_seq",
    )(zq, zq, zq, zq, log_a, g_gla.reshape(1, GLA_DV))


def _to_column(x_row, eye):
    return jnp.sum(jnp.where(eye, x_row, 0.0), axis=1, keepdims=True)


def _gla_step_kernel(zq_ref, la_ref, g_ref, s0_ref, xb_ref, snew_ref):
    eye = (lax.broadcasted_iota(jnp.int32, (GLA_DK, GLA_DK), 0)
           == lax.broadcasted_iota(jnp.int32, (GLA_DK, GLA_DK), 1))
    pad_rows = 2 * SUBLANES

    def one_sequence(s, carry):
        for h in range(GLA_HEADS):
            def cols(base, width=GLA_DK, h=h):
                return slice(base + h * width, base + (h + 1) * width)
            q = zq_ref[s, :, cols(0)] * (GLA_DK ** -0.5)
            k = zq_ref[s, :, cols(GLA_KEY)]
            v = zq_ref[s, :, cols(2 * GLA_KEY)]
            r = zq_ref[s, :, cols(2 * GLA_KEY + GLA_VAL)]
            b = la_ref[s, :, cols(0)]
            qt = q * jnp.exp(b)
            kt = k * jnp.exp(-b)
            a = jnp.exp(b)
            s0 = s0_ref[s, h]
            score = jnp.sum(qt * kt, axis=-1, keepdims=True)
            o_inter = jnp.dot(jnp.broadcast_to(qt, (pad_rows, GLA_DK)).astype(BF16),
                              s0.astype(BF16), preferred_element_type=F32)[0:1, :]
            o = score * v + o_inter
            snew_ref[s, h] = s0 * _to_column(a, eye) + _to_column(k, eye) * v
            ms = jnp.mean(o * o, axis=-1, keepdims=True)
            y = (o * lax.rsqrt(ms + EPS)) * g_ref[...]
            xb_ref[s, :, cols(0, GLA_DV)] = (y * _silu(r)).astype(BF16)
        return carry

    lax.fori_loop(0, zq_ref.shape[0], one_sequence, 0)


def _gla_step(zq, log_a, g_gla, s0):
    n = zq.shape[0]
    wq = zq.shape[1]
    per_step = GLA_STEP_SEQS if n % GLA_STEP_SEQS == 0 else 1
    sblk = (per_step, GLA_HEADS, GLA_DK, GLA_DV)
    est = 4 * _nbytes(sblk, F32) + 8 * GLA_DK * GLA_DV * 4
    xb, s_new = pl.pallas_call(
        _gla_step_kernel,
        grid=(n // per_step,),
        in_specs=[pl.BlockSpec((per_step, 1, wq), lambda b: (b, 0, 0)),
                  pl.BlockSpec((per_step, 1, GLA_KEY), lambda b: (b, 0, 0)),
                  pl.BlockSpec((1, GLA_DV), lambda b: (0, 0)),
                  pl.BlockSpec(sblk, lambda b: (b, 0, 0, 0))],
        out_specs=[pl.BlockSpec((per_step, 1, GLA_VAL), lambda b: (b, 0, 0)),
                   pl.BlockSpec(sblk, lambda b: (b, 0, 0, 0))],
        out_shape=[jax.ShapeDtypeStruct((n, 1, GLA_VAL), BF16),
                   jax.ShapeDtypeStruct(s0.shape, F32)],
        compiler_params=pltpu.CompilerParams(
            dimension_semantics=("arbitrary",),
            vmem_limit_bytes=_vmem_limit(est)),
        name="gla_step",
    )(zq.reshape(n, 1, wq), log_a.reshape(n, 1, GLA_KEY), g_gla.reshape(1, GLA_DV), s0)
    return xb.reshape(n, GLA_VAL), s_new


_PLAN = dict(
    norm=512,
    short_conv=dict(tm=1024, tn=256),
    qkvr=dict(tm=1024, tn=512),
    log_decay=512,
    merge=dict(tm=512, tn=256),
    mix_out=dict(tm=1024, tn=512),
    ffn_up=dict(tm=1024, tn=256),
    ffn_down=dict(tm=512, tn=512),
    ple=dict(tm=1024, tn=512),
)


def kernel(x_prompt, x_sample, p_prompt, p_sample, state_conv, state_gla, state_ffn, g_mix, w_in,
           w_alpha2, b_alpha, w_conv, w_out_conv, g_gla, w_out_gla, w_mix_out, g_ffn, w_up,
           w_ffn_conv, w_down, g_ple, w_ple_gate, w_ple_proj, g_final):
    assert g_mix.shape[0] == 1, "single-layer trunk"
    assert x_sample.shape[1] == 1, "the sample group advances one token per sequence"
    n_seq, seq_len, d = x_prompt.shape
    n_step = x_sample.shape[0]
    plan = _PLAN
    w_in_t = jnp.swapaxes(w_in[0], 0, 1)
    w_alr_t = w_in_t[OFF_ALR:OFF_GATE_A]
    w_down_b = w_down[0].astype(BF16)
    conv_state, gla_state, ffn_state = state_conv[0], state_gla[0], state_ffn[0]

    h0_q = x_prompt.reshape(n_seq * seq_len, d)
    h0_s = x_sample.reshape(n_step, d)
    p_q = _interleave_rows(p_prompt[0].reshape(n_seq * seq_len, -1).astype(BF16),
                           plan["ple"]["tm"])
    p_s = p_sample[0].reshape(n_step, -1).astype(BF16)

    n1_q = _rmsnorm(h0_q, g_mix[0], BF16, plan["norm"])
    n1_s = _rmsnorm(h0_s, g_mix[0], BF16, plan["norm"])
    zq_q, zq_s = _qkvr_proj(n1_q, n1_s, w_in_t, **plan["qkvr"])
    la_q = _log_decay(n1_q, w_alr_t, w_alpha2[0], b_alpha[0], tm=plan["log_decay"])
    la_s = _log_decay(n1_s, w_alr_t, w_alpha2[0], b_alpha[0], tm=plan["log_decay"])
    (xa_q, conv_last), (xa_s, u_s) = _short_conv_branch(
        n1_q, n1_s, w_in_t, w_conv[0], conv_state.reshape(n_step, -1), seq_len=seq_len,
        **plan["short_conv"])
    xb_q, gla_q = _gla_seq(zq_q, la_q, g_gla[0], n_seq=n_seq, seq_len=seq_len)
    xb_s, gla_s = _gla_step(zq_s, la_s, g_gla[0], gla_state)

    mg_q, mg_s = _merge([n1_q, xa_q, xb_q], [n1_s, xa_s, xb_s], w_in_t, w_out_conv[0],
                        w_out_gla[0], **plan["merge"])
    assert plan["mix_out"]["tm"] == plan["ffn_up"]["tm"] == plan["ple"]["tm"]
    h1_q, h1_s = _proj_residual("mix_out", mg_q, h0_q, mg_s, h0_s, w_mix_out[0],
                                interleave_out=True, **plan["mix_out"])

    n2_q = _rmsnorm(h1_q, g_ffn[0], BF16, plan["norm"])
    n2_s = _rmsnorm(h1_s, g_ffn[0], BF16, plan["norm"])
    (act_q, last_g, last_v), (act_s, ug_s, uv_s) = _conv_ffn_up(
        n2_q, n2_s, w_up[0], w_ffn_conv[0], ffn_state.reshape(n_step, -1), seq_len=seq_len,
        **plan["ffn_up"])
    h2_q, h2_s = _proj_residual("ffn_down", act_q, h1_q, act_s, h1_s, w_down_b,
                                **plan["ffn_down"])

    n3_q = _rmsnorm(h2_q, g_ple[0], BF16, plan["norm"])
    n3_s = _rmsnorm(h2_s, g_ple[0], BF16, plan["norm"])
    h3_q, h3_s = _ple([n3_q, p_q], h2_q, [n3_s, p_s], h2_s, w_ple_gate[0], w_ple_proj[0],
                      **plan["ple"])
    y_q = _rmsnorm(h3_q, g_final, F32, plan["norm"]).reshape(n_seq, seq_len, d)
    y_s = _rmsnorm(h3_s, g_final, F32, plan["norm"]).reshape(n_step, 1, d)

    keep = slice(SUBLANES - (CONV_W - 1), SUBLANES)
    conv_q = conv_last[:, keep, :]
    ffn_q = jnp.concatenate([last_g, last_v], axis=-1)[:, keep, :]
    conv_s = jnp.stack([conv_state[:, 1, :], u_s], axis=1)
    ffn_s = jnp.stack([ffn_state[:, 1, :], jnp.concatenate([ug_s, uv_s], axis=-1)], axis=1)
    return (y_q, y_s, conv_q[None], conv_s[None], gla_q[None], gla_s[None], ffn_q[None],
            ffn_s[None])
```

```python
import functools

import jax
import jax.numpy as jnp
from jax import lax
from jax.experimental import pallas as pl
from jax.experimental.pallas import tpu as pltpu

F32 = jnp.float32
BF16 = jnp.bfloat16

D_MODEL = 4096
D_CONV = 2048
CONV_W = 3
GLA_HEADS = 8
GLA_DK = 256
GLA_DV = 256
GLA_KEY = GLA_HEADS * GLA_DK
GLA_VAL = GLA_HEADS * GLA_DV
GATE_RANK = 16
GATE_NORM = 16.0
GLA_CHUNK = 32
D_FF = 11008
EPS = 1e-6

V7X_VMEM_CAP = 60 * 2**20
SUBLANES = 8
LANES = 128
MXU_DIM = 256
GLA_SUB = MXU_DIM
GLA_STEP_SEQS = 4
CAST_CHUNK_ELEMS = 256 * 1024

OFF_HA = 0
OFF_CA = D_CONV
OFF_BA = 2 * D_CONV
OFF_QKVR = 3 * D_CONV
OFF_ALR = OFF_QKVR + 2 * GLA_KEY + 2 * GLA_VAL
OFF_GATE_A = OFF_ALR + GATE_RANK
OFF_GATE_B = OFF_GATE_A + D_MODEL

_NT = (((1,), (1,)), ((), ()))
_TN = (((0,), (0,)), ((), ()))


def _vmem_limit(nbytes):
    return int(min(V7X_VMEM_CAP, nbytes + 8 * 2**20))


def _nbytes(shape, dtype):
    n = 1
    for s in shape:
        n *= int(s)
    return n * jnp.dtype(dtype).itemsize


def _largest_divisor(n, limit, multiple):
    best = multiple
    for d in range(multiple, min(n, limit) + 1, multiple):
        if n % d == 0:
            best = d
    assert n % best == 0
    return best


def _rmsnorm_kernel(x_ref, g_ref, o_ref):
    x = x_ref[...]
    ms = jnp.mean(x * x, axis=-1, keepdims=True)
    o_ref[...] = ((x * lax.rsqrt(ms + EPS)) * g_ref[...]).astype(o_ref.dtype)


def _rmsnorm(x, g, out_dtype, tm):
    m, d = x.shape
    tm = min(m, tm)
    return pl.pallas_call(
        _rmsnorm_kernel,
        grid=(m // tm,),
        in_specs=[pl.BlockSpec((tm, d), lambda i: (i, 0)),
                  pl.BlockSpec((1, d), lambda i: (0, 0))],
        out_specs=pl.BlockSpec((tm, d), lambda i: (i, 0)),
        out_shape=jax.ShapeDtypeStruct((m, d), out_dtype),
        compiler_params=pltpu.CompilerParams(
            dimension_semantics=("arbitrary",),
            vmem_limit_bytes=_vmem_limit(6 * tm * d * 4)),
        name="rmsnorm",
    )(x, g.reshape(1, d))


def _cast_rows(src_ref, dst_ref):
    rows, cols = src_ref.shape
    ck = _largest_divisor(rows, max(2 * SUBLANES, CAST_CHUNK_ELEMS // cols), 2 * SUBLANES)

    def body(c, carry):
        r = pl.multiple_of(c * ck, ck)
        dst_ref[pl.ds(r, ck), :] = src_ref[pl.ds(r, ck), :].astype(BF16)
        return carry

    lax.fori_loop(0, rows // ck, body, 0)


def _fused_matmul(name, a_list, w_list, extra_list, out_list, epilogue, *, tm, tn, nj,
                  scratch=(), acc_history=False, zero_scratch=False, tail=None):
    m = a_list[0].shape[0]
    assert m % tm == 0
    ni = m // tm
    t_a, t_extra, t_out, t_epilogue = tail if tail is not None else ([], [], [], None)
    assert len(t_a) in (0, len(a_list))
    na, nw, ne, no = len(a_list), len(w_list), len(extra_list), len(out_list)
    nta, nte, nto = len(t_a), len(t_extra), len(t_out)
    cast_ids = [n for n, w in enumerate(w_list) if w[0].dtype != BF16]

    def w_block(w, off, transposed):
        if transposed:
            assert off % SUBLANES == 0 and tn % SUBLANES == 0
            return ((pl.Element(tn), pl.Element(w.shape[1])),
                    (lambda j, i: (pl.multiple_of(off + j * tn, SUBLANES), 0)))
        assert off % tn == 0
        return (w.shape[0], tn), (lambda j, i: (0, off // tn + j))

    def w_tile_shape(w, transposed):
        return (tn, w.shape[1]) if transposed else (w.shape[0], tn)

    in_specs = [pl.BlockSpec((tm, a.shape[1]), lambda j, i: (i, 0)) for a in a_list]
    in_specs += [pl.BlockSpec(a.shape, lambda j, i: (0, 0)) for a in t_a]
    in_specs += [pl.BlockSpec(*w_block(w, off, tr)) for w, _, off, tr in w_list]
    in_specs += [pl.BlockSpec(blk, im) for _, blk, im in list(extra_list) + list(t_extra)]
    out_specs = [pl.BlockSpec(blk, im) for _, blk, im in list(out_list) + list(t_out)]
    out_shape = [s for s, _, _ in list(out_list) + list(t_out)]
    scratch_shapes = [pltpu.VMEM(w_tile_shape(w_list[n][0], w_list[n][3]), BF16) for n in cast_ids]
    n_acc = nw if acc_history else 0
    scratch_shapes += [pltpu.VMEM((2 * SUBLANES + tm, tn), F32)] * n_acc
    scratch_shapes += list(scratch)

    def kernel(*refs):
        pos = 0

        def take(count):
            nonlocal pos
            pos += count
            return refs[pos - count:pos]

        a_refs, ta_refs, w_refs = take(na), take(nta), take(nw)
        e_refs, te_refs, o_refs, to_refs = take(ne), take(nte), take(no), take(nto)
        s_refs = refs[pos:]
        wb_refs = dict(zip(cast_ids, s_refs[:len(cast_ids)]))
        acc_refs = s_refs[len(cast_ids):len(cast_ids) + n_acc]
        user_scratch = s_refs[len(cast_ids) + n_acc:]
        j = pl.program_id(0)
        i = pl.program_id(1)

        @pl.when(i == 0)
        def _():
            for n in cast_ids:
                _cast_rows(w_refs[n], wb_refs[n])
            for ref in list(acc_refs) + (list(user_scratch) if zero_scratch else []):
                ref[...] = jnp.zeros_like(ref)

        def dots(lhs_refs):
            accs = []
            for n, (_, ai, _, transposed) in enumerate(w_list):
                w_ref = wb_refs[n] if n in wb_refs else w_refs[n]
                a = lhs_refs[ai][...]
                if transposed:
                    accs.append(lax.dot_general(a, w_ref[...], _NT, preferred_element_type=F32))
                else:
                    accs.append(jnp.dot(a, w_ref[...], preferred_element_type=F32))
            return accs

        accs = dots(a_refs)
        if acc_history:
            for n in range(nw):
                _push_interleaved(acc_refs[n], accs[n])
            accs = list(acc_refs)
        epilogue(accs, e_refs, o_refs, user_scratch, j, i)

        if t_epilogue is not None:
            @pl.when(i == ni - 1)
            def _():
                t_epilogue(dots(ta_refs), te_refs, to_refs, j)

    est = 0
    for a in a_list:
        est += 2 * _nbytes((tm, a.shape[1]), a.dtype)
    for a in t_a:
        est += 2 * _nbytes(a.shape, a.dtype)
    for w, _, _, tr in w_list:
        est += 2 * _nbytes(w_tile_shape(w, tr), w.dtype)
        if w.dtype != BF16:
            est += _nbytes(w_tile_shape(w, tr), BF16)
    for arr, blk, _ in list(extra_list) + list(out_list) + list(t_extra) + list(t_out):
        est += 2 * _nbytes(blk, arr.dtype)
    est += 3 * nw * tm * tn * 4

    outs = pl.pallas_call(
        kernel,
        grid=(nj, ni),
        in_specs=in_specs,
        out_specs=out_specs,
        out_shape=out_shape,
        scratch_shapes=scratch_shapes,
        compiler_params=pltpu.CompilerParams(
            dimension_semantics=("arbitrary", "arbitrary"),
            vmem_limit_bytes=_vmem_limit(est)),
        name=name,
    )(*a_list, *t_a, *[w[0] for w in w_list], *[e for e, _, _ in extra_list],
      *[e for e, _, _ in t_extra])
    return outs[:no], outs[no:]


def _tile(tm, tn):
    return (tm, tn), (lambda j, i: (i, j))


def _tail_tile(rows, tn, off_blocks=0):
    return (rows, tn), (lambda j, i: (0, off_blocks + j))


def _push_history(hist_ref, tile):
    prev_tail = _history_tail(hist_ref)
    hist_ref[SUBLANES:SUBLANES + tile.shape[0], :] = tile
    hist_ref[0:SUBLANES, :] = prev_tail


def _conv3_history(hist_ref, seq_start, w3):
    tm = hist_ref.shape[0] - SUBLANES
    hist_ref[0:SUBLANES, :] = jnp.where(seq_start, 0.0, hist_ref[0:SUBLANES, :])
    taps = [hist_ref[SUBLANES - (CONV_W - 1) + t:SUBLANES - (CONV_W - 1) + t + tm, :]
            for t in range(CONV_W)]
    return w3[0:1, :] * taps[0] + w3[1:2, :] * taps[1] + w3[2:3, :] * taps[2]


def _history_tail(hist_ref):
    tm = hist_ref.shape[0] - SUBLANES
    return hist_ref[tm:tm + SUBLANES, :]


def _stage_shape(tm, tn):
    return (tn // LANES, tm, LANES)


def _store_interleaved(o_ref, stage_ref, natural):
    nv = natural.shape[0] // SUBLANES
    for c in range(stage_ref.shape[0]):
        lanes = slice(c * LANES, (c + 1) * LANES)
        for s in range(SUBLANES):
            stage_ref[c, pl.ds(s, nv, stride=SUBLANES), :] = natural[s * nv:(s + 1) * nv, lanes]
        o_ref[:, lanes] = stage_ref[c]


def _store_natural(o_ref, stage_ref, interleaved):
    nv = interleaved.shape[0] // SUBLANES
    for c in range(stage_ref.shape[0]):
        lanes = slice(c * LANES, (c + 1) * LANES)
        stage_ref[c] = interleaved[:, lanes]
        for s in range(SUBLANES):
            o_ref[s * nv:(s + 1) * nv, lanes] = stage_ref[c, pl.ds(s, nv, stride=SUBLANES), :]


def _interleave_rows(x, group):
    m, n = x.shape
    nv = group // SUBLANES
    return x.reshape(m // group, SUBLANES, nv, n).swapaxes(1, 2).reshape(m, n)


def _push_interleaved(hist_ref, tile):
    tm = tile.shape[0]
    two = 2 * SUBLANES
    prev_last = hist_ref[tm:tm + two, :]
    hist_ref[two:two + tm, :] = tile
    row = lax.broadcasted_iota(jnp.int32, (SUBLANES, tile.shape[1]), 0)
    for k in range(2):
        cur = tile[tm - two + SUBLANES * k:tm - SUBLANES + SUBLANES * k, :]
        carry = prev_last[SUBLANES * k + SUBLANES - 1:SUBLANES * (k + 1), :]
        hist_ref[SUBLANES * k:SUBLANES * (k + 1), :] = jnp.where(
            row == 0, carry, pltpu.roll(cur, 1, axis=0))


def _conv3_interleaved(hist_ref, seq_start, w3):
    two = 2 * SUBLANES
    tm = hist_ref.shape[0] - two
    row = lax.broadcasted_iota(jnp.int32, (two, hist_ref.shape[1]), 0)
    hist_ref[0:two, :] = jnp.where(seq_start & (row % SUBLANES == 0), 0.0, hist_ref[0:two, :])
    taps = [hist_ref[SUBLANES * t:SUBLANES * t + tm, :] for t in range(CONV_W)]
    return w3[0:1, :] * taps[0] + w3[1:2, :] * taps[1] + w3[2:3, :] * taps[2]


def _last_two_tokens(hist_ref):
    tm = hist_ref.shape[0] - 2 * SUBLANES
    u_a = hist_ref[tm:tm + SUBLANES, :]
    u_b = hist_ref[tm + SUBLANES:tm + 2 * SUBLANES, :]
    row = lax.broadcasted_iota(jnp.int32, u_a.shape, 0)
    return jnp.where(row == SUBLANES - 2, pltpu.roll(u_a, SUBLANES - 1, axis=0), u_b)


def _conv3_step(w3, prev2, prev1, cur):
    return w3[0:1, :] * prev2 + w3[1:2, :] * prev1 + w3[2:3, :] * cur


def _silu(x):
    return x * jax.nn.sigmoid(x)


def _short_conv_branch(n1_q, n1_s, w_in_t, w_conv, state2d, *, seq_len, tm, tn):
    m, ms = n1_q.shape[0], n1_s.shape[0]
    nj = D_CONV // tn
    tps = seq_len // tm
    n_seq = m // seq_len
    w_list = [(w_in_t, 0, OFF_HA, True), (w_in_t, 0, OFF_CA, True), (w_in_t, 0, OFF_BA, True)]
    wc = (w_conv, (CONV_W, tn), lambda j, i: (0, j))

    def epilogue(accs, e_refs, o_refs, s_refs, j, i):
        h, c, b = accs
        (wc_ref,), (xa_ref, last_ref), (u_hist,) = e_refs, o_refs, s_refs
        _push_history(u_hist, c * h)
        cu = _conv3_history(u_hist, i % tps == 0, wc_ref[...])
        xa_ref[...] = (b * cu).astype(BF16)
        last_ref[0] = _history_tail(u_hist)

    def step_epilogue(accs, e_refs, o_refs, j):
        h, c, b = accs
        wc_ref, s0_ref, s1_ref = e_refs
        xa_ref, u_ref = o_refs
        u = c * h
        xa_ref[...] = (b * _conv3_step(wc_ref[...], s0_ref[...], s1_ref[...], u)).astype(BF16)
        u_ref[...] = u

    outs = [(jax.ShapeDtypeStruct((m, D_CONV), BF16),) + _tile(tm, tn),
            (jax.ShapeDtypeStruct((n_seq, SUBLANES, D_CONV), F32), (1, SUBLANES, tn),
             lambda j, i: (i // tps, 0, j))]
    tail = ([n1_s],
            [wc, (state2d,) + _tail_tile(ms, tn), (state2d,) + _tail_tile(ms, tn, nj)],
            [(jax.ShapeDtypeStruct((ms, D_CONV), BF16),) + _tail_tile(ms, tn),
             (jax.ShapeDtypeStruct((ms, D_CONV), F32),) + _tail_tile(ms, tn)],
            step_epilogue)
    return _fused_matmul("short_conv", [n1_q], w_list, [wc], outs, epilogue,
                         tm=tm, tn=tn, nj=nj, scratch=[pltpu.VMEM((SUBLANES + tm, tn), F32)],
                         zero_scratch=True, tail=tail)


def _qkvr_proj(n1_q, n1_s, w_in_t, *, tm, tn):
    m, ms = n1_q.shape[0], n1_s.shape[0]
    n = 2 * GLA_KEY + 2 * GLA_VAL

    def epilogue(accs, e_refs, o_refs, s_refs, j, i):
        o_refs[0][...] = accs[0]

    def step_epilogue(accs, e_refs, o_refs, j):
        o_refs[0][...] = accs[0]

    outs = [(jax.ShapeDtypeStruct((m, n), F32),) + _tile(tm, tn)]
    tail = ([n1_s], [], [(jax.ShapeDtypeStruct((ms, n), F32),) + _tail_tile(ms, tn)],
            step_epilogue)
    (zq,), (zq_s,) = _fused_matmul("qkvr_proj", [n1_q], [(w_in_t, 0, OFF_QKVR, True)], [], outs,
                                   epilogue, tm=tm, tn=tn, nj=n // tn, tail=tail)
    return zq, zq_s


def _log_decay_kernel(n_ref, w1_ref, w2_ref, b_ref, o_ref):
    a_lr = lax.dot_general(n_ref[...], w1_ref[...], _NT, preferred_element_type=F32)
    x = jnp.dot(a_lr.astype(BF16), w2_ref[...], preferred_element_type=F32) + b_ref[...]
    log_sig = jnp.minimum(x, 0.0) - jnp.log1p(jnp.exp(-jnp.abs(x)))
    o_ref[...] = log_sig / GATE_NORM


def _log_decay(n1, w_alr_t, w_alpha2, b_alpha, *, tm):
    m, d = n1.shape
    tm = min(m, tm)
    w1 = jnp.pad(w_alr_t, ((0, LANES - GATE_RANK), (0, 0))).astype(BF16)
    w2 = jnp.pad(w_alpha2, ((0, LANES - GATE_RANK), (0, 0))).astype(BF16)
    return pl.pallas_call(
        _log_decay_kernel,
        grid=(m // tm,),
        in_specs=[pl.BlockSpec((tm, d), lambda i: (i, 0)),
                  pl.BlockSpec((LANES, d), lambda i: (0, 0)),
                  pl.BlockSpec((LANES, GLA_KEY), lambda i: (0, 0)),
                  pl.BlockSpec((1, GLA_KEY), lambda i: (0, 0))],
        out_specs=pl.BlockSpec((tm, GLA_KEY), lambda i: (i, 0)),
        out_shape=jax.ShapeDtypeStruct((m, GLA_KEY), F32),
        compiler_params=pltpu.CompilerParams(
            dimension_semantics=("arbitrary",),
            vmem_limit_bytes=_vmem_limit(2 * tm * d * 2 + 6 * tm * GLA_KEY * 4 + 4 * 2**20)),
        name="log_decay",
    )(n1, w1, w2, b_alpha.reshape(1, GLA_KEY))


def _merge(lhs_q, lhs_s, w_in_t, w_out_conv, w_out_gla, *, tm, tn):
    m, ms = lhs_q[0].shape[0], lhs_s[0].shape[0]

    def merged(accs):
        ga, gb, ya, yb = accs
        return (jax.nn.sigmoid(ga) * ya + jax.nn.sigmoid(gb) * yb).astype(BF16)

    def epilogue(accs, e_refs, o_refs, s_refs, j, i):
        o_refs[0][...] = merged(accs)

    def step_epilogue(accs, e_refs, o_refs, j):
        o_refs[0][...] = merged(accs)

    w_list = [(w_in_t, 0, OFF_GATE_A, True), (w_in_t, 0, OFF_GATE_B, True),
              (w_out_conv, 1, 0, False), (w_out_gla, 2, 0, False)]
    outs = [(jax.ShapeDtypeStruct((m, D_MODEL), BF16),) + _tile(tm, tn)]
    tail = (list(lhs_s), [], [(jax.ShapeDtypeStruct((ms, D_MODEL), BF16),) + _tail_tile(ms, tn)],
            step_epilogue)
    (out,), (out_s,) = _fused_matmul("gated_merge", list(lhs_q), w_list, [], outs, epilogue,
                                     tm=tm, tn=tn, nj=D_MODEL // tn, tail=tail)
    return out, out_s


def _proj_residual(name, a_q, res_q, a_s, res_s, w, *, tm, tn, interleave_out=False):
    m, ms = a_q.shape[0], a_s.shape[0]
    n = w.shape[1]

    def epilogue(accs, e_refs, o_refs, s_refs, j, i):
        if interleave_out:
            _store_interleaved(o_refs[0], s_refs[0], e_refs[0][...] + accs[0])
        else:
            o_refs[0][...] = e_refs[0][...] + accs[0]

    def step_epilogue(accs, e_refs, o_refs, j):
        o_refs[0][...] = e_refs[0][...] + accs[0]

    extras = [(res_q,) + _tile(tm, tn)]
    outs = [(jax.ShapeDtypeStruct((m, n), F32),) + _tile(tm, tn)]
    tail = ([a_s], [(res_s,) + _tail_tile(ms, tn)],
            [(jax.ShapeDtypeStruct((ms, n), F32),) + _tail_tile(ms, tn)], step_epilogue)
    scratch = [pltpu.VMEM(_stage_shape(tm, tn), F32)] if interleave_out else []
    (out,), (out_s,) = _fused_matmul(name, [a_q], [(w, 0, 0, False)], extras, outs, epilogue,
                                     tm=tm, tn=tn, nj=n // tn, tail=tail, scratch=scratch)
    return out, out_s


def _conv_ffn_up(n2_q, n2_s, w_up, w_ffn_conv, state2d, *, seq_len, tm, tn):
    m, ms = n2_q.shape[0], n2_s.shape[0]
    nj = D_FF // tn
    tps = seq_len // tm
    n_seq = m // seq_len
    w_list = [(w_up, 0, 0, False), (w_up, 0, D_FF, False)]
    wcg = (w_ffn_conv, (CONV_W, tn), lambda j, i: (0, j))
    wcv = (w_ffn_conv, (CONV_W, tn), lambda j, i: (0, nj + j))

    def epilogue(accs, e_refs, o_refs, s_refs, j, i):
        ug_hist, uv_hist = accs
        wg_ref, wv_ref = e_refs
        act_ref, lastg_ref, lastv_ref = o_refs
        seq_start = i % tps == 0
        cg = _conv3_interleaved(ug_hist, seq_start, wg_ref[...])
        cv = _conv3_interleaved(uv_hist, seq_start, wv_ref[...])
        act_ref[...] = (_silu(cg) * cv).astype(BF16)
        lastg_ref[0] = _last_two_tokens(ug_hist)
        lastv_ref[0] = _last_two_tokens(uv_hist)

    def step_epilogue(accs, e_refs, o_refs, j):
        ug, uv = accs
        wg_ref, wv_ref, g0, v0, g1, v1 = e_refs
        act_ref, ug_ref, uv_ref = o_refs
        cg = _conv3_step(wg_ref[...], g0[...], g1[...], ug)
        cv = _conv3_step(wv_ref[...], v0[...], v1[...], uv)
        act_ref[...] = (_silu(cg) * cv).astype(BF16)
        ug_ref[...] = ug
        uv_ref[...] = uv

    last = (jax.ShapeDtypeStruct((n_seq, SUBLANES, D_FF), F32), (1, SUBLANES, tn),
            lambda j, i: (i // tps, 0, j))
    outs = [(jax.ShapeDtypeStruct((m, D_FF), BF16),) + _tile(tm, tn), last, last]
    up_s = (jax.ShapeDtypeStruct((ms, D_FF), F32),) + _tail_tile(ms, tn)
    tail = ([n2_s],
            [wcg, wcv] + [(state2d,) + _tail_tile(ms, tn, q * nj) for q in range(4)],
            [(jax.ShapeDtypeStruct((ms, D_FF), BF16),) + _tail_tile(ms, tn), up_s, up_s],
            step_epilogue)
    return _fused_matmul("conv_ffn_up", [n2_q], w_list, [wcg, wcv], outs, epilogue,
                         tm=tm, tn=tn, nj=nj, acc_history=True, tail=tail)


def _ple(lhs_q, h2_q, lhs_s, h2_s, w_ple_gate, w_ple_proj, *, tm, tn):
    m, ms = h2_q.shape[0], h2_s.shape[0]

    def epilogue(accs, e_refs, o_refs, s_refs, j, i):
        gate, proj = accs
        _store_natural(o_refs[0], s_refs[0], e_refs[0][...] + jax.nn.sigmoid(gate) * proj)

    def step_epilogue(accs, e_refs, o_refs, j):
        gate, proj = accs
        o_refs[0][...] = e_refs[0][...] + jax.nn.sigmoid(gate) * proj

    extras = [(h2_q,) + _tile(tm, tn)]
    outs = [(jax.ShapeDtypeStruct((m, D_MODEL), F32),) + _tile(tm, tn)]
    tail = (list(lhs_s), [(h2_s,) + _tail_tile(ms, tn)],
            [(jax.ShapeDtypeStruct((ms, D_MODEL), F32),) + _tail_tile(ms, tn)], step_epilogue)
    (out,), (out_s,) = _fused_matmul(
        "ple_gate", list(lhs_q), [(w_ple_gate, 0, 0, False), (w_ple_proj, 1, 0, False)], extras,
        outs, epilogue, tm=tm, tn=tn, nj=D_MODEL // tn, tail=tail,
        scratch=[pltpu.VMEM(_stage_shape(tm, tn), F32)])
    return out, out_s


def _gla_seq_kernel(q_ref, k_ref, v_ref, r_ref, la_ref, g_ref, xb_ref, sfin_ref,
                    st_ref, b_sc, cum_sc, *, seq_len):
    sub, c = GLA_SUB, GLA_CHUNK
    n_chunks = sub // c
    st_ref[...] = jnp.zeros_like(st_ref)
    row = lax.broadcasted_iota(jnp.int32, (sub, sub), 0)
    col = lax.broadcasted_iota(jnp.int32, (sub, sub), 1)
    row_in_chunk = row % c
    groups = [g for g in (2 * c, 4 * c, 8 * c) if g <= sub]

    def sub_block(sb, carry):
        r0 = pl.multiple_of(sb * sub, sub)
        b = la_ref[pl.ds(r0, sub), :]
        s = 1
        while s < c:
            b = b + jnp.where(row_in_chunk >= s, pltpu.roll(b, s, axis=0), 0.0)
            s *= 2
        b_sc[...] = b
        offset = jnp.zeros((1, GLA_DK), F32)
        for n in range(n_chunks):
            cum_sc[n * c:(n + 1) * c, :] = b[n * c:(n + 1) * c, :] + offset
            offset = offset + b_sc[n * c + c - 1:n * c + c, :]
        cum = cum_sc[...]
        cum_last = offset

        q = q_ref[pl.ds(r0, sub), :] * (GLA_DK ** -0.5)
        k = k_ref[pl.ds(r0, sub), :]
        vb = v_ref[pl.ds(r0, sub), :].astype(BF16)

        qt = (q * jnp.exp(b)).astype(BF16)
        kt = (k * jnp.exp(-b)).astype(BF16)
        diag = ((row // c) == (col // c)) & (col <= row)
        scores = jnp.where(diag, lax.dot_general(qt, kt, _NT, preferred_element_type=F32), 0.0)
        for g in groups:
            half = g // 2
            mid = jnp.concatenate(
                [jnp.broadcast_to(cum_sc[grp * g + half - 1:grp * g + half, :], (g, GLA_DK))
                 for grp in range(sub // g)], axis=0)
            factor = jnp.exp(jnp.where(row % g >= half, cum - mid, mid - cum))
            cross = lax.dot_general((q * factor).astype(BF16), (k * factor).astype(BF16), _NT,
                                    preferred_element_type=F32)
            pair = ((row // g) == (col // g)) & (row % g >= half) & (col % g < half)
            scores = jnp.where(pair, cross, scores)
        o = jnp.dot(scores.astype(BF16), vb, preferred_element_type=F32)

        st = st_ref[...]
        o = o + lax.dot_general((q * jnp.exp(cum)).astype(BF16), st.astype(BF16), _NT,
                                preferred_element_type=F32)
        k_end = (k * jnp.exp(cum_last - cum)).astype(BF16)
        st_ref[...] = st * jnp.exp(cum_last) + lax.dot_general(vb, k_end, _TN,
                                                                preferred_element_type=F32)
        ms = jnp.mean(o * o, axis=-1, keepdims=True)
        y = (o * lax.rsqrt(ms + EPS)) * g_ref[...]
        r = r_ref[pl.ds(r0, sub), :]
        xb_ref[pl.ds(r0, sub), :] = (y * _silu(r)).astype(BF16)
        return carry

    lax.fori_loop(0, seq_len // sub, sub_block, 0)
    sfin_ref[0, 0] = st_ref[...].T


def _gla_seq(zq, log_a, g_gla, *, n_seq, seq_len):
    h = GLA_HEADS
    blk = (seq_len, GLA_DK)

    def col(off):
        return pl.BlockSpec(blk, functools.partial(lambda b, hh, off: (b, off + hh), off=off))

    est = 2 * 5 * _nbytes(blk, F32) + 2 * _nbytes(blk, BF16) + 16 * GLA_SUB * GLA_SUB * 4
    return pl.pallas_call(
        functools.partial(_gla_seq_kernel, seq_len=seq_len),
        grid=(n_seq, h),
        in_specs=[col(0), col(h), col(2 * h), col(3 * h),
                  pl.BlockSpec(blk, lambda b, hh: (b, hh)),
                  pl.BlockSpec((1, GLA_DV), lambda b, hh: (0, 0))],
        out_specs=[pl.BlockSpec(blk, lambda b, hh: (b, hh)),
                   pl.BlockSpec((1, 1, GLA_DK, GLA_DV), lambda b, hh: (b, hh, 0, 0))],
        out_shape=[jax.ShapeDtypeStruct((n_seq * seq_len, GLA_VAL), BF16),
                   jax.ShapeDtypeStruct((n_seq, h, GLA_DK, GLA_DV), F32)],
        scratch_shapes=[pltpu.VMEM((GLA_DV, GLA_DK), F32),
                        pltpu.VMEM((GLA_SUB, GLA_DK), F32),
                        pltpu.VMEM((GLA_SUB, GLA_DK), F32)],
        compiler_params=pltpu.CompilerParams(
            dimension_semantics=("arbitrary", "arbitrary"),
            vmem_limit_bytes=_vmem_limit(est)),
        name="gla_seq",
    )(zq, zq, zq, zq, log_a, g_gla.reshape(1, GLA_DV))


def _to_column(x_row, eye):
    return jnp.sum(jnp.where(eye, x_row, 0.0), axis=1, keepdims=True)


def _gla_step_kernel(zq_ref, la_ref, g_ref, s0_ref, xb_ref, snew_ref):
    eye = (lax.broadcasted_iota(jnp.int32, (GLA_DK, GLA_DK), 0)
           == lax.broadcasted_iota(jnp.int32, (GLA_DK, GLA_DK), 1))
    pad_rows = 2 * SUBLANES

    def one_sequence(s, carry):
        for h in range(GLA_HEADS):
            def cols(base, width=GLA_DK, h=h):
                return slice(base + h * width, base + (h + 1) * width)
            q = zq_ref[s, :, cols(0)] * (GLA_DK ** -0.5)
            k = zq_ref[s, :, cols(GLA_KEY)]
            v = zq_ref[s, :, cols(2 * GLA_KEY)]
            r = zq_ref[s, :, cols(2 * GLA_KEY + GLA_VAL)]
            b = la_ref[s, :, cols(0)]
            qt = q * jnp.exp(b)
            kt = k * jnp.exp(-b)
            a = jnp.exp(b)
            s0 = s0_ref[s, h]
            score = jnp.sum(qt * kt, axis=-1, keepdims=True)
            o_inter = jnp.dot(jnp.broadcast_to(qt, (pad_rows, GLA_DK)).astype(BF16),
                              s0.astype(BF16), preferred_element_type=F32)[0:1, :]
            o = score * v + o_inter
            snew_ref[s, h] = s0 * _to_column(a, eye) + _to_column(k, eye) * v
            ms = jnp.mean(o * o, axis=-1, keepdims=True)
            y = (o * lax.rsqrt(ms + EPS)) * g_ref[...]
            xb_ref[s, :, cols(0, GLA_DV)] = (y * _silu(r)).astype(BF16)
        return carry

    lax.fori_loop(0, zq_ref.shape[0], one_sequence, 0)


def _gla_step(zq, log_a, g_gla, s0):
    n = zq.shape[0]
    wq = zq.shape[1]
    per_step = GLA_STEP_SEQS if n % GLA_STEP_SEQS == 0 else 1
    sblk = (per_step, GLA_HEADS, GLA_DK, GLA_DV)
    est = 4 * _nbytes(sblk, F32) + 8 * GLA_DK * GLA_DV * 4
    xb, s_new = pl.pallas_call(
        _gla_step_kernel,
        grid=(n // per_step,),
        in_specs=[pl.BlockSpec((per_step, 1, wq), lambda b: (b, 0, 0)),
                  pl.BlockSpec((per_step, 1, GLA_KEY), lambda b: (b, 0, 0)),
                  pl.BlockSpec((1, GLA_DV), lambda b: (0, 0)),
                  pl.BlockSpec(sblk, lambda b: (b, 0, 0, 0))],
        out_specs=[pl.BlockSpec((per_step, 1, GLA_VAL), lambda b: (b, 0, 0)),
                   pl.BlockSpec(sblk, lambda b: (b, 0, 0, 0))],
        out_shape=[jax.ShapeDtypeStruct((n, 1, GLA_VAL), BF16),
                   jax.ShapeDtypeStruct(s0.shape, F32)],
        compiler_params=pltpu.CompilerParams(
            dimension_semantics=("arbitrary",),
            vmem_limit_bytes=_vmem_limit(est)),
        name="gla_step",
    )(zq.reshape(n, 1, wq), log_a.reshape(n, 1, GLA_KEY), g_gla.reshape(1, GLA_DV), s0)
    return xb.reshape(n, GLA_VAL), s_new


_PLAN = dict(
    norm=512,
    short_conv=dict(tm=1024, tn=256),
    qkvr=dict(tm=1024, tn=512),
    log_decay=512,
    merge=dict(tm=512, tn=256),
    mix_out=dict(tm=1024, tn=512),
    ffn_up=dict(tm=1024, tn=256),
    ffn_down=dict(tm=512, tn=512),
    ple=dict(tm=1024, tn=512),
)


def kernel(x_prompt, x_sample, p_prompt, p_sample, state_conv, state_gla, state_ffn, g_mix, w_in,
           w_alpha2, b_alpha, w_conv, w_out_conv, g_gla, w_out_gla, w_mix_out, g_ffn, w_up,
           w_ffn_conv, w_down, g_ple, w_ple_gate, w_ple_proj, g_final):
    assert g_mix.shape[0] == 1, "single-layer trunk"
    assert x_sample.shape[1] == 1, "the sample group advances one token per sequence"
    n_seq, seq_len, d = x_prompt.shape
    n_step = x_sample.shape[0]
    plan = _PLAN
    w_in_t = jnp.swapaxes(w_in[0], 0, 1)
    w_alr_t = w_in_t[OFF_ALR:OFF_GATE_A]
    w_down_b = w_down[0].astype(BF16)
    conv_state, gla_state, ffn_state = state_conv[0], state_gla[0], state_ffn[0]

    h0_q = x_prompt.reshape(n_seq * seq_len, d)
    h0_s = x_sample.reshape(n_step, d)
    p_q = _interleave_rows(p_prompt[0].reshape(n_seq * seq_len, -1).astype(BF16),
                           plan["ple"]["tm"])
    p_s = p_sample[0].reshape(n_step, -1).astype(BF16)

    n1_q = _rmsnorm(h0_q, g_mix[0], BF16, plan["norm"])
    n1_s = _rmsnorm(h0_s, g_mix[0], BF16, plan["norm"])
    zq_q, zq_s = _qkvr_proj(n1_q, n1_s, w_in_t, **plan["qkvr"])
    la_q = _log_decay(n1_q, w_alr_t, w_alpha2[0], b_alpha[0], tm=plan["log_decay"])
    la_s = _log_decay(n1_s, w_alr_t, w_alpha2[0], b_alpha[0], tm=plan["log_decay"])
    (xa_q, conv_last), (xa_s, u_s) = _short_conv_branch(
        n1_q, n1_s, w_in_t, w_conv[0], conv_state.reshape(n_step, -1), seq_len=seq_len,
        **plan["short_conv"])
    xb_q, gla_q = _gla_seq(zq_q, la_q, g_gla[0], n_seq=n_seq, seq_len=seq_len)
    xb_s, gla_s = _gla_step(zq_s, la_s, g_gla[0], gla_state)

    mg_q, mg_s = _merge([n1_q, xa_q, xb_q], [n1_s, xa_s, xb_s], w_in_t, w_out_conv[0],
                        w_out_gla[0], **plan["merge"])
    assert plan["mix_out"]["tm"] == plan["ffn_up"]["tm"] == plan["ple"]["tm"]
    h1_q, h1_s = _proj_residual("mix_out", mg_q, h0_q, mg_s, h0_s, w_mix_out[0],
                                interleave_out=True, **plan["mix_out"])

    n2_q = _rmsnorm(h1_q, g_ffn[0], BF16, plan["norm"])
    n2_s = _rmsnorm(h1_s, g_ffn[0], BF16, plan["norm"])
    (act_q, last_g, last_v), (act_s, ug_s, uv_s) = _conv_ffn_up(
        n2_q, n2_s, w_up[0], w_ffn_conv[0], ffn_state.reshape(n_step, -1), seq_len=seq_len,
        **plan["ffn_up"])
    h2_q, h2_s = _proj_residual("ffn_down", act_q, h1_q, act_s, h1_s, w_down_b,
                                **plan["ffn_down"])

    n3_q = _rmsnorm(h2_q, g_ple[0], BF16, plan["norm"])
    n3_s = _rmsnorm(h2_s, g_ple[0], BF16, plan["norm"])
    h3_q, h3_s = _ple([n3_q, p_q], h2_q, [n3_s, p_s], h2_s, w_ple_gate[0], w_ple_proj[0],
                      **plan["ple"])
    y_q = _rmsnorm(h3_q, g_final, F32, plan["norm"]).reshape(n_seq, seq_len, d)
    y_s = _rmsnorm(h3_s, g_final, F32, plan["norm"]).reshape(n_step, 1, d)

    keep = slice(SUBLANES - (CONV_W - 1), SUBLANES)
    conv_q = conv_last[:, keep, :]
    ffn_q = jnp.concatenate([last_g, last_v], axis=-1)[:, keep, :]
    conv_s = jnp.stack([conv_state[:, 1, :], u_s], axis=1)
    ffn_s = jnp.stack([ffn_state[:, 1, :], jnp.concatenate([ug_s, uv_s], axis=-1)], axis=1)
    return (y_q, y_s, conv_q[None], conv_s[None], gla_q[None], gla_s[None], ffn_q[None],
            ffn_s[None])
```

```python
import functools

import jax
import jax.numpy as jnp
from jax import lax
from jax.experimental import pallas as pl
from jax.experimental.pallas import tpu as pltpu

F32 = jnp.float32
BF16 = jnp.bfloat16

D_MODEL = 4096
D_CONV = 2048
CONV_W = 3
GLA_HEADS = 8
GLA_DK = 256
GLA_DV = 256
GLA_KEY = GLA_HEADS * GLA_DK
GLA_VAL = GLA_HEADS * GLA_DV
GATE_RANK = 16
GATE_NORM = 16.0
GLA_CHUNK = 32
D_FF = 11008
EPS = 1e-6

V7X_VMEM_CAP = 60 * 2**20
SUBLANES = 8
LANES = 128
MXU_DIM = 256
GLA_SUB = MXU_DIM
GLA_STEP_SEQS = 4
CAST_CHUNK_ELEMS = 256 * 1024

OFF_HA = 0
OFF_CA = D_CONV
OFF_BA = 2 * D_CONV
OFF_QKVR = 3 * D_CONV
OFF_ALR = OFF_QKVR + 2 * GLA_KEY + 2 * GLA_VAL
OFF_GATE_A = OFF_ALR + GATE_RANK
OFF_GATE_B = OFF_GATE_A + D_MODEL

_NT = (((1,), (1,)), ((), ()))
_TN = (((0,), (0,)), ((), ()))


def _vmem_limit(nbytes):
    return int(min(V7X_VMEM_CAP, nbytes + 8 * 2**20))


def _nbytes(shape, dtype):
    n = 1
    for s in shape:
        n *= int(s)
    return n * jnp.dtype(dtype).itemsize


def _largest_divisor(n, limit, multiple):
    best = multiple
    for d in range(multiple, min(n, limit) + 1, multiple):
        if n % d == 0:
            best = d
    assert n % best == 0
    return best


def _rmsnorm_kernel(x_ref, g_ref, o_ref):
    x = x_ref[...]
    ms = jnp.mean(x * x, axis=-1, keepdims=True)
    o_ref[...] = ((x * lax.rsqrt(ms + EPS)) * g_ref[...]).astype(o_ref.dtype)


def _rmsnorm(x, g, out_dtype, tm):
    m, d = x.shape
    tm = min(m, tm)
    return pl.pallas_call(
        _rmsnorm_kernel,
        grid=(m // tm,),
        in_specs=[pl.BlockSpec((tm, d), lambda i: (i, 0)),
                  pl.BlockSpec((1, d), lambda i: (0, 0))],
        out_specs=pl.BlockSpec((tm, d), lambda i: (i, 0)),
        out_shape=jax.ShapeDtypeStruct((m, d), out_dtype),
        compiler_params=pltpu.CompilerParams(
            dimension_semantics=("arbitrary",),
            vmem_limit_bytes=_vmem_limit(6 * tm * d * 4)),
        name="rmsnorm",
    )(x, g.reshape(1, d))


def _cast_rows(src_ref, dst_ref):
    rows, cols = src_ref.shape
    ck = _largest_divisor(rows, max(2 * SUBLANES, CAST_CHUNK_ELEMS // cols), 2 * SUBLANES)

    def body(c, carry):
        r = pl.multiple_of(c * ck, ck)
        dst_ref[pl.ds(r, ck), :] = src_ref[pl.ds(r, ck), :].astype(BF16)
        return carry

    lax.fori_loop(0, rows // ck, body, 0)


def _fused_matmul(name, a_list, w_list, extra_list, out_list, epilogue, *, tm, tn, nj,
                  scratch=(), acc_history=False, zero_scratch=False, tail=None,
                  single_buffer_weights=False):
    m = a_list[0].shape[0]
    assert m % tm == 0
    ni = m // tm
    t_a, t_extra, t_out, t_epilogue = tail if tail is not None else ([], [], [], None)
    assert len(t_a) in (0, len(a_list))
    na, nw, ne, no = len(a_list), len(w_list), len(extra_list), len(out_list)
    nta, nte, nto = len(t_a), len(t_extra), len(t_out)
    cast_ids = [n for n, w in enumerate(w_list) if w[0].dtype != BF16]

    def w_block(w, off, transposed):
        if transposed:
            assert off % SUBLANES == 0 and tn % SUBLANES == 0
            return ((pl.Element(tn), pl.Element(w.shape[1])),
                    (lambda j, i: (pl.multiple_of(off + j * tn, SUBLANES), 0)))
        assert off % tn == 0
        return (w.shape[0], tn), (lambda j, i: (0, off // tn + j))

    def w_tile_shape(w, transposed):
        return (tn, w.shape[1]) if transposed else (w.shape[0], tn)

    in_specs = [pl.BlockSpec((tm, a.shape[1]), lambda j, i: (i, 0)) for a in a_list]
    in_specs += [pl.BlockSpec(a.shape, lambda j, i: (0, 0)) for a in t_a]
    w_mode = dict(pipeline_mode=pl.Buffered(1)) if single_buffer_weights else {}
    in_specs += [pl.BlockSpec(*w_block(w, off, tr), **w_mode) for w, _, off, tr in w_list]
    in_specs += [pl.BlockSpec(blk, im) for _, blk, im in list(extra_list) + list(t_extra)]
    out_specs = [pl.BlockSpec(blk, im) for _, blk, im in list(out_list) + list(t_out)]
    out_shape = [s for s, _, _ in list(out_list) + list(t_out)]
    scratch_shapes = [pltpu.VMEM(w_tile_shape(w_list[n][0], w_list[n][3]), BF16) for n in cast_ids]
    n_acc = nw if acc_history else 0
    scratch_shapes += [pltpu.VMEM((2 * SUBLANES + tm, tn), F32)] * n_acc
    scratch_shapes += list(scratch)

    def kernel(*refs):
        pos = 0

        def take(count):
            nonlocal pos
            pos += count
            return refs[pos - count:pos]

        a_refs, ta_refs, w_refs = take(na), take(nta), take(nw)
        e_refs, te_refs, o_refs, to_refs = take(ne), take(nte), take(no), take(nto)
        s_refs = refs[pos:]
        wb_refs = dict(zip(cast_ids, s_refs[:len(cast_ids)]))
        acc_refs = s_refs[len(cast_ids):len(cast_ids) + n_acc]
        user_scratch = s_refs[len(cast_ids) + n_acc:]
        j = pl.program_id(0)
        i = pl.program_id(1)

        @pl.when(i == 0)
        def _():
            for n in cast_ids:
                _cast_rows(w_refs[n], wb_refs[n])
            for ref in list(acc_refs) + (list(user_scratch) if zero_scratch else []):
                ref[...] = jnp.zeros_like(ref)

        def dots(lhs_refs):
            accs = []
            for n, (_, ai, _, transposed) in enumerate(w_list):
                w_ref = wb_refs[n] if n in wb_refs else w_refs[n]
                a = lhs_refs[ai][...]
                if transposed:
                    accs.append(lax.dot_general(a, w_ref[...], _NT, preferred_element_type=F32))
                else:
                    accs.append(jnp.dot(a, w_ref[...], preferred_element_type=F32))
            return accs

        accs = dots(a_refs)
        if acc_history:
            for n in range(nw):
                _push_interleaved(acc_refs[n], accs[n])
            accs = list(acc_refs)
        epilogue(accs, e_refs, o_refs, user_scratch, j, i)

        if t_epilogue is not None:
            @pl.when(i == ni - 1)
            def _():
                t_epilogue(dots(ta_refs), te_refs, to_refs, j)

    est = 0
    for a in a_list:
        est += 2 * _nbytes((tm, a.shape[1]), a.dtype)
    for a in t_a:
        est += 2 * _nbytes(a.shape, a.dtype)
    for w, _, _, tr in w_list:
        est += (1 if single_buffer_weights else 2) * _nbytes(w_tile_shape(w, tr), w.dtype)
        if w.dtype != BF16:
            est += _nbytes(w_tile_shape(w, tr), BF16)
    for arr, blk, _ in list(extra_list) + list(out_list) + list(t_extra) + list(t_out):
        est += 2 * _nbytes(blk, arr.dtype)
    est += 3 * nw * tm * tn * 4

    outs = pl.pallas_call(
        kernel,
        grid=(nj, ni),
        in_specs=in_specs,
        out_specs=out_specs,
        out_shape=out_shape,
        scratch_shapes=scratch_shapes,
        compiler_params=pltpu.CompilerParams(
            dimension_semantics=("arbitrary", "arbitrary"),
            vmem_limit_bytes=_vmem_limit(est)),
        name=name,
    )(*a_list, *t_a, *[w[0] for w in w_list], *[e for e, _, _ in extra_list],
      *[e for e, _, _ in t_extra])
    return outs[:no], outs[no:]


def _tile(tm, tn):
    return (tm, tn), (lambda j, i: (i, j))


def _tail_tile(rows, tn, off_blocks=0):
    return (rows, tn), (lambda j, i: (0, off_blocks + j))


def _push_history(hist_ref, tile):
    prev_tail = _history_tail(hist_ref)
    hist_ref[SUBLANES:SUBLANES + tile.shape[0], :] = tile
    hist_ref[0:SUBLANES, :] = prev_tail


def _conv3_history(hist_ref, seq_start, w3):
    tm = hist_ref.shape[0] - SUBLANES
    hist_ref[0:SUBLANES, :] = jnp.where(seq_start, 0.0, hist_ref[0:SUBLANES, :])
    taps = [hist_ref[SUBLANES - (CONV_W - 1) + t:SUBLANES - (CONV_W - 1) + t + tm, :]
            for t in range(CONV_W)]
    return w3[0:1, :] * taps[0] + w3[1:2, :] * taps[1] + w3[2:3, :] * taps[2]


def _history_tail(hist_ref):
    tm = hist_ref.shape[0] - SUBLANES
    return hist_ref[tm:tm + SUBLANES, :]


def _stage_shape(tm, tn):
    return (tn // LANES, tm, LANES)


def _store_interleaved(o_ref, stage_ref, natural):
    nv = natural.shape[0] // SUBLANES
    for c in range(stage_ref.shape[0]):
        lanes = slice(c * LANES, (c + 1) * LANES)
        for s in range(SUBLANES):
            stage_ref[c, pl.ds(s, nv, stride=SUBLANES), :] = natural[s * nv:(s + 1) * nv, lanes]
        o_ref[:, lanes] = stage_ref[c]


def _store_natural(o_ref, stage_ref, interleaved):
    nv = interleaved.shape[0] // SUBLANES
    for c in range(stage_ref.shape[0]):
        lanes = slice(c * LANES, (c + 1) * LANES)
        stage_ref[c] = interleaved[:, lanes]
        for s in range(SUBLANES):
            o_ref[s * nv:(s + 1) * nv, lanes] = stage_ref[c, pl.ds(s, nv, stride=SUBLANES), :]


def _interleave_rows(x, group):
    m, n = x.shape
    nv = group // SUBLANES
    return x.reshape(m // group, SUBLANES, nv, n).swapaxes(1, 2).reshape(m, n)


def _push_interleaved(hist_ref, tile):
    tm = tile.shape[0]
    two = 2 * SUBLANES
    prev_last = hist_ref[tm:tm + two, :]
    hist_ref[two:two + tm, :] = tile
    row = lax.broadcasted_iota(jnp.int32, (SUBLANES, tile.shape[1]), 0)
    for k in range(2):
        cur = tile[tm - two + SUBLANES * k:tm - SUBLANES + SUBLANES * k, :]
        carry = prev_last[SUBLANES * k + SUBLANES - 1:SUBLANES * (k + 1), :]
        hist_ref[SUBLANES * k:SUBLANES * (k + 1), :] = jnp.where(
            row == 0, carry, pltpu.roll(cur, 1, axis=0))


def _conv3_interleaved(hist_ref, seq_start, w3):
    two = 2 * SUBLANES
    tm = hist_ref.shape[0] - two
    row = lax.broadcasted_iota(jnp.int32, (two, hist_ref.shape[1]), 0)
    hist_ref[0:two, :] = jnp.where(seq_start & (row % SUBLANES == 0), 0.0, hist_ref[0:two, :])
    taps = [hist_ref[SUBLANES * t:SUBLANES * t + tm, :] for t in range(CONV_W)]
    return w3[0:1, :] * taps[0] + w3[1:2, :] * taps[1] + w3[2:3, :] * taps[2]


def _last_two_tokens(hist_ref):
    tm = hist_ref.shape[0] - 2 * SUBLANES
    u_a = hist_ref[tm:tm + SUBLANES, :]
    u_b = hist_ref[tm + SUBLANES:tm + 2 * SUBLANES, :]
    row = lax.broadcasted_iota(jnp.int32, u_a.shape, 0)
    return jnp.where(row == SUBLANES - 2, pltpu.roll(u_a, SUBLANES - 1, axis=0), u_b)


def _conv3_step(w3, prev2, prev1, cur):
    return w3[0:1, :] * prev2 + w3[1:2, :] * prev1 + w3[2:3, :] * cur


def _silu(x):
    return x * jax.nn.sigmoid(x)


def _short_conv_branch(n1_q, n1_s, w_in_t, w_conv, state2d, *, seq_len, tm, tn):
    m, ms = n1_q.shape[0], n1_s.shape[0]
    nj = D_CONV // tn
    tps = seq_len // tm
    n_seq = m // seq_len
    w_list = [(w_in_t, 0, OFF_HA, True), (w_in_t, 0, OFF_CA, True), (w_in_t, 0, OFF_BA, True)]
    wc = (w_conv, (CONV_W, tn), lambda j, i: (0, j))

    def epilogue(accs, e_refs, o_refs, s_refs, j, i):
        h, c, b = accs
        (wc_ref,), (xa_ref, last_ref), (u_hist,) = e_refs, o_refs, s_refs
        _push_history(u_hist, c * h)
        cu = _conv3_history(u_hist, i % tps == 0, wc_ref[...])
        xa_ref[...] = (b * cu).astype(BF16)
        last_ref[0] = _history_tail(u_hist)

    def step_epilogue(accs, e_refs, o_refs, j):
        h, c, b = accs
        wc_ref, s0_ref, s1_ref = e_refs
        xa_ref, u_ref = o_refs
        u = c * h
        xa_ref[...] = (b * _conv3_step(wc_ref[...], s0_ref[...], s1_ref[...], u)).astype(BF16)
        u_ref[...] = u

    outs = [(jax.ShapeDtypeStruct((m, D_CONV), BF16),) + _tile(tm, tn),
            (jax.ShapeDtypeStruct((n_seq, SUBLANES, D_CONV), F32), (1, SUBLANES, tn),
             lambda j, i: (i // tps, 0, j))]
    tail = ([n1_s],
            [wc, (state2d,) + _tail_tile(ms, tn), (state2d,) + _tail_tile(ms, tn, nj)],
            [(jax.ShapeDtypeStruct((ms, D_CONV), BF16),) + _tail_tile(ms, tn),
             (jax.ShapeDtypeStruct((ms, D_CONV), F32),) + _tail_tile(ms, tn)],
            step_epilogue)
    return _fused_matmul("short_conv", [n1_q], w_list, [wc], outs, epilogue,
                         tm=tm, tn=tn, nj=nj, scratch=[pltpu.VMEM((SUBLANES + tm, tn), F32)],
                         zero_scratch=True, tail=tail)


def _qkvr_proj(n1_q, n1_s, w_in_t, *, tm, tn):
    m, ms = n1_q.shape[0], n1_s.shape[0]
    n = 2 * GLA_KEY + 2 * GLA_VAL

    def epilogue(accs, e_refs, o_refs, s_refs, j, i):
        o_refs[0][...] = accs[0]

    def step_epilogue(accs, e_refs, o_refs, j):
        o_refs[0][...] = accs[0]

    outs = [(jax.ShapeDtypeStruct((m, n), F32),) + _tile(tm, tn)]
    tail = ([n1_s], [], [(jax.ShapeDtypeStruct((ms, n), F32),) + _tail_tile(ms, tn)],
            step_epilogue)
    (zq,), (zq_s,) = _fused_matmul("qkvr_proj", [n1_q], [(w_in_t, 0, OFF_QKVR, True)], [], outs,
                                   epilogue, tm=tm, tn=tn, nj=n // tn, tail=tail,
                                   single_buffer_weights=True)
    return zq, zq_s


def _log_decay_kernel(n_ref, w1_ref, w2_ref, b_ref, o_ref):
    a_lr = lax.dot_general(n_ref[...], w1_ref[...], _NT, preferred_element_type=F32)
    x = jnp.dot(a_lr.astype(BF16), w2_ref[...], preferred_element_type=F32) + b_ref[...]
    log_sig = jnp.minimum(x, 0.0) - jnp.log1p(jnp.exp(-jnp.abs(x)))
    o_ref[...] = log_sig / GATE_NORM


def _log_decay(n1, w_alr_t, w_alpha2, b_alpha, *, tm):
    m, d = n1.shape
    tm = min(m, tm)
    w1 = jnp.pad(w_alr_t, ((0, LANES - GATE_RANK), (0, 0))).astype(BF16)
    w2 = jnp.pad(w_alpha2, ((0, LANES - GATE_RANK), (0, 0))).astype(BF16)
    return pl.pallas_call(
        _log_decay_kernel,
        grid=(m // tm,),
        in_specs=[pl.BlockSpec((tm, d), lambda i: (i, 0)),
                  pl.BlockSpec((LANES, d), lambda i: (0, 0)),
                  pl.BlockSpec((LANES, GLA_KEY), lambda i: (0, 0)),
                  pl.BlockSpec((1, GLA_KEY), lambda i: (0, 0))],
        out_specs=pl.BlockSpec((tm, GLA_KEY), lambda i: (i, 0)),
        out_shape=jax.ShapeDtypeStruct((m, GLA_KEY), F32),
        compiler_params=pltpu.CompilerParams(
            dimension_semantics=("arbitrary",),
            vmem_limit_bytes=_vmem_limit(2 * tm * d * 2 + 6 * tm * GLA_KEY * 4 + 4 * 2**20)),
        name="log_decay",
    )(n1, w1, w2, b_alpha.reshape(1, GLA_KEY))


def _merge(lhs_q, lhs_s, w_in_t, w_out_conv, w_out_gla, *, tm, tn):
    m, ms = lhs_q[0].shape[0], lhs_s[0].shape[0]

    def merged(accs):
        ga, gb, ya, yb = accs
        return (jax.nn.sigmoid(ga) * ya + jax.nn.sigmoid(gb) * yb).astype(BF16)

    def epilogue(accs, e_refs, o_refs, s_refs, j, i):
        o_refs[0][...] = merged(accs)

    def step_epilogue(accs, e_refs, o_refs, j):
        o_refs[0][...] = merged(accs)

    w_list = [(w_in_t, 0, OFF_GATE_A, True), (w_in_t, 0, OFF_GATE_B, True),
              (w_out_conv, 1, 0, False), (w_out_gla, 2, 0, False)]
    outs = [(jax.ShapeDtypeStruct((m, D_MODEL), BF16),) + _tile(tm, tn)]
    tail = (list(lhs_s), [], [(jax.ShapeDtypeStruct((ms, D_MODEL), BF16),) + _tail_tile(ms, tn)],
            step_epilogue)
    (out,), (out_s,) = _fused_matmul("gated_merge", list(lhs_q), w_list, [], outs, epilogue,
                                     tm=tm, tn=tn, nj=D_MODEL // tn, tail=tail)
    return out, out_s


def _proj_residual(name, a_q, res_q, a_s, res_s, w, *, tm, tn, interleave_out=False):
    m, ms = a_q.shape[0], a_s.shape[0]
    n = w.shape[1]

    def epilogue(accs, e_refs, o_refs, s_refs, j, i):
        if interleave_out:
            _store_interleaved(o_refs[0], s_refs[0], e_refs[0][...] + accs[0])
        else:
            o_refs[0][...] = e_refs[0][...] + accs[0]

    def step_epilogue(accs, e_refs, o_refs, j):
        o_refs[0][...] = e_refs[0][...] + accs[0]

    extras = [(res_q,) + _tile(tm, tn)]
    outs = [(jax.ShapeDtypeStruct((m, n), F32),) + _tile(tm, tn)]
    tail = ([a_s], [(res_s,) + _tail_tile(ms, tn)],
            [(jax.ShapeDtypeStruct((ms, n), F32),) + _tail_tile(ms, tn)], step_epilogue)
    scratch = [pltpu.VMEM(_stage_shape(tm, tn), F32)] if interleave_out else []
    (out,), (out_s,) = _fused_matmul(name, [a_q], [(w, 0, 0, False)], extras, outs, epilogue,
                                     tm=tm, tn=tn, nj=n // tn, tail=tail, scratch=scratch)
    return out, out_s


def _conv_ffn_up(n2_q, n2_s, w_up, w_ffn_conv, state2d, *, seq_len, tm, tn):
    m, ms = n2_q.shape[0], n2_s.shape[0]
    nj = D_FF // tn
    tps = seq_len // tm
    n_seq = m // seq_len
    w_list = [(w_up, 0, 0, False), (w_up, 0, D_FF, False)]
    wcg = (w_ffn_conv, (CONV_W, tn), lambda j, i: (0, j))
    wcv = (w_ffn_conv, (CONV_W, tn), lambda j, i: (0, nj + j))

    def epilogue(accs, e_refs, o_refs, s_refs, j, i):
        ug_hist, uv_hist = accs
        wg_ref, wv_ref = e_refs
        act_ref, lastg_ref, lastv_ref = o_refs
        seq_start = i % tps == 0
        cg = _conv3_interleaved(ug_hist, seq_start, wg_ref[...])
        cv = _conv3_interleaved(uv_hist, seq_start, wv_ref[...])
        act_ref[...] = (_silu(cg) * cv).astype(BF16)
        lastg_ref[0] = _last_two_tokens(ug_hist)
        lastv_ref[0] = _last_two_tokens(uv_hist)

    def step_epilogue(accs, e_refs, o_refs, j):
        ug, uv = accs
        wg_ref, wv_ref, g0, v0, g1, v1 = e_refs
        act_ref, ug_ref, uv_ref = o_refs
        cg = _conv3_step(wg_ref[...], g0[...], g1[...], ug)
        cv = _conv3_step(wv_ref[...], v0[...], v1[...], uv)
        act_ref[...] = (_silu(cg) * cv).astype(BF16)
        ug_ref[...] = ug
        uv_ref[...] = uv

    last = (jax.ShapeDtypeStruct((n_seq, SUBLANES, D_FF), F32), (1, SUBLANES, tn),
            lambda j, i: (i // tps, 0, j))
    outs = [(jax.ShapeDtypeStruct((m, D_FF), BF16),) + _tile(tm, tn), last, last]
    up_s = (jax.ShapeDtypeStruct((ms, D_FF), F32),) + _tail_tile(ms, tn)
    tail = ([n2_s],
            [wcg, wcv] + [(state2d,) + _tail_tile(ms, tn, q * nj) for q in range(4)],
            [(jax.ShapeDtypeStruct((ms, D_FF), BF16),) + _tail_tile(ms, tn), up_s, up_s],
            step_epilogue)
    return _fused_matmul("conv_ffn_up", [n2_q], w_list, [wcg, wcv], outs, epilogue,
                         tm=tm, tn=tn, nj=nj, acc_history=True, tail=tail)


def _ple(lhs_q, h2_q, lhs_s, h2_s, w_ple_gate, w_ple_proj, *, tm, tn):
    m, ms = h2_q.shape[0], h2_s.shape[0]

    def epilogue(accs, e_refs, o_refs, s_refs, j, i):
        gate, proj = accs
        _store_natural(o_refs[0], s_refs[0], e_refs[0][...] + jax.nn.sigmoid(gate) * proj)

    def step_epilogue(accs, e_refs, o_refs, j):
        gate, proj = accs
        o_refs[0][...] = e_refs[0][...] + jax.nn.sigmoid(gate) * proj

    extras = [(h2_q,) + _tile(tm, tn)]
    outs = [(jax.ShapeDtypeStruct((m, D_MODEL), F32),) + _tile(tm, tn)]
    tail = (list(lhs_s), [(h2_s,) + _tail_tile(ms, tn)],
            [(jax.ShapeDtypeStruct((ms, D_MODEL), F32),) + _tail_tile(ms, tn)], step_epilogue)
    (out,), (out_s,) = _fused_matmul(
        "ple_gate", list(lhs_q), [(w_ple_gate, 0, 0, False), (w_ple_proj, 1, 0, False)], extras,
        outs, epilogue, tm=tm, tn=tn, nj=D_MODEL // tn, tail=tail,
        scratch=[pltpu.VMEM(_stage_shape(tm, tn), F32)])
    return out, out_s


def _gla_seq_kernel(q_ref, k_ref, v_ref, r_ref, la_ref, g_ref, xb_ref, sfin_ref,
                    st_ref, b_sc, cum_sc, *, seq_len):
    sub, c = GLA_SUB, GLA_CHUNK
    n_chunks = sub // c
    st_ref[...] = jnp.zeros_like(st_ref)
    row = lax.broadcasted_iota(jnp.int32, (sub, sub), 0)
    col = lax.broadcasted_iota(jnp.int32, (sub, sub), 1)
    row_in_chunk = row % c
    groups = [g for g in (2 * c, 4 * c, 8 * c) if g <= sub]

    def sub_block(sb, carry):
        r0 = pl.multiple_of(sb * sub, sub)
        b = la_ref[pl.ds(r0, sub), :]
        s = 1
        while s < c:
            b = b + jnp.where(row_in_chunk >= s, pltpu.roll(b, s, axis=0), 0.0)
            s *= 2
        b_sc[...] = b
        offset = jnp.zeros((1, GLA_DK), F32)
        for n in range(n_chunks):
            cum_sc[n * c:(n + 1) * c, :] = b[n * c:(n + 1) * c, :] + offset
            offset = offset + b_sc[n * c + c - 1:n * c + c, :]
        cum = cum_sc[...]
        cum_last = offset

        q = q_ref[pl.ds(r0, sub), :] * (GLA_DK ** -0.5)
        k = k_ref[pl.ds(r0, sub), :]
        vb = v_ref[pl.ds(r0, sub), :].astype(BF16)

        qt = (q * jnp.exp(b)).astype(BF16)
        kt = (k * jnp.exp(-b)).astype(BF16)
        diag = ((row // c) == (col // c)) & (col <= row)
        scores = jnp.where(diag, lax.dot_general(qt, kt, _NT, preferred_element_type=F32), 0.0)
        for g in groups:
            half = g // 2
            mid = jnp.concatenate(
                [jnp.broadcast_to(cum_sc[grp * g + half - 1:grp * g + half, :], (g, GLA_DK))
                 for grp in range(sub // g)], axis=0)
            factor = jnp.exp(jnp.where(row % g >= half, cum - mid, mid - cum))
            cross = lax.dot_general((q * factor).astype(BF16), (k * factor).astype(BF16), _NT,
                                    preferred_element_type=F32)
            pair = ((row // g) == (col // g)) & (row % g >= half) & (col % g < half)
            scores = jnp.where(pair, cross, scores)
        o = jnp.dot(scores.astype(BF16), vb, preferred_element_type=F32)

        st = st_ref[...]
        o = o + lax.dot_general((q * jnp.exp(cum)).astype(BF16), st.astype(BF16), _NT,
                                preferred_element_type=F32)
        k_end = (k * jnp.exp(cum_last - cum)).astype(BF16)
        st_ref[...] = st * jnp.exp(cum_last) + lax.dot_general(vb, k_end, _TN,
                                                                preferred_element_type=F32)
        ms = jnp.mean(o * o, axis=-1, keepdims=True)
        y = (o * lax.rsqrt(ms + EPS)) * g_ref[...]
        r = r_ref[pl.ds(r0, sub), :]
        xb_ref[pl.ds(r0, sub), :] = (y * _silu(r)).astype(BF16)
        return carry

    lax.fori_loop(0, seq_len // sub, sub_block, 0)
    sfin_ref[0, 0] = st_ref[...].T


def _gla_seq(zq, log_a, g_gla, *, n_seq, seq_len):
    h = GLA_HEADS
    blk = (seq_len, GLA_DK)

    def col(off):
        return pl.BlockSpec(blk, functools.partial(lambda b, hh, off: (b, off + hh), off=off))

    est = 2 * 5 * _nbytes(blk, F32) + 2 * _nbytes(blk, BF16) + 16 * GLA_SUB * GLA_SUB * 4
    return pl.pallas_call(
        functools.partial(_gla_seq_kernel, seq_len=seq_len),
        grid=(n_seq, h),
        in_specs=[col(0), col(h), col(2 * h), col(3 * h),
                  pl.BlockSpec(blk, lambda b, hh: (b, hh)),
                  pl.BlockSpec((1, GLA_DV), lambda b, hh: (0, 0))],
        out_specs=[pl.BlockSpec(blk, lambda b, hh: (b, hh)),
                   pl.BlockSpec((1, 1, GLA_DK, GLA_DV), lambda b, hh: (b, hh, 0, 0))],
        out_shape=[jax.ShapeDtypeStruct((n_seq * seq_len, GLA_VAL), BF16),
                   jax.ShapeDtypeStruct((n_seq, h, GLA_DK, GLA_DV), F32)],
        scratch_shapes=[pltpu.VMEM((GLA_DV, GLA_DK), F32),
                        pltpu.VMEM((GLA_SUB, GLA_DK), F32),
                        pltpu.VMEM((GLA_SUB, GLA_DK), F32)],
        compiler_params=pltpu.CompilerParams(
            dimension_semantics=("arbitrary", "arbitrary"),
            vmem_limit_bytes=_vmem_limit(est)),
        name="gla_seq",
    )(zq, zq, zq, zq, log_a, g_gla.reshape(1, GLA_DV))


def _to_column(x_row, eye):
    return jnp.sum(jnp.where(eye, x_row, 0.0), axis=1, keepdims=True)


def _gla_step_kernel(zq_ref, la_ref, g_ref, s0_ref, xb_ref, snew_ref):
    eye = (lax.broadcasted_iota(jnp.int32, (GLA_DK, GLA_DK), 0)
           == lax.broadcasted_iota(jnp.int32, (GLA_DK, GLA_DK), 1))
    pad_rows = 2 * SUBLANES

    def one_sequence(s, carry):
        for h in range(GLA_HEADS):
            def cols(base, width=GLA_DK, h=h):
                return slice(base + h * width, base + (h + 1) * width)
            q = zq_ref[s, :, cols(0)] * (GLA_DK ** -0.5)
            k = zq_ref[s, :, cols(GLA_KEY)]
            v = zq_ref[s, :, cols(2 * GLA_KEY)]
            r = zq_ref[s, :, cols(2 * GLA_KEY + GLA_VAL)]
            b = la_ref[s, :, cols(0)]
            qt = q * jnp.exp(b)
            kt = k * jnp.exp(-b)
            a = jnp.exp(b)
            s0 = s0_ref[s, h]
            score = jnp.sum(qt * kt, axis=-1, keepdims=True)
            o_inter = jnp.dot(jnp.broadcast_to(qt, (pad_rows, GLA_DK)).astype(BF16),
                              s0.astype(BF16), preferred_element_type=F32)[0:1, :]
            o = score * v + o_inter
            snew_ref[s, h] = s0 * _to_column(a, eye) + _to_column(k, eye) * v
            ms = jnp.mean(o * o, axis=-1, keepdims=True)
            y = (o * lax.rsqrt(ms + EPS)) * g_ref[...]
            xb_ref[s, :, cols(0, GLA_DV)] = (y * _silu(r)).astype(BF16)
        return carry

    lax.fori_loop(0, zq_ref.shape[0], one_sequence, 0)


def _gla_step(zq, log_a, g_gla, s0):
    n = zq.shape[0]
    wq = zq.shape[1]
    per_step = GLA_STEP_SEQS if n % GLA_STEP_SEQS == 0 else 1
    sblk = (per_step, GLA_HEADS, GLA_DK, GLA_DV)
    est = 4 * _nbytes(sblk, F32) + 8 * GLA_DK * GLA_DV * 4
    xb, s_new = pl.pallas_call(
        _gla_step_kernel,
        grid=(n // per_step,),
        in_specs=[pl.BlockSpec((per_step, 1, wq), lambda b: (b, 0, 0)),
                  pl.BlockSpec((per_step, 1, GLA_KEY), lambda b: (b, 0, 0)),
                  pl.BlockSpec((1, GLA_DV), lambda b: (0, 0)),
                  pl.BlockSpec(sblk, lambda b: (b, 0, 0, 0))],
        out_specs=[pl.BlockSpec((per_step, 1, GLA_VAL), lambda b: (b, 0, 0)),
                   pl.BlockSpec(sblk, lambda b: (b, 0, 0, 0))],
        out_shape=[jax.ShapeDtypeStruct((n, 1, GLA_VAL), BF16),
                   jax.ShapeDtypeStruct(s0.shape, F32)],
        compiler_params=pltpu.CompilerParams(
            dimension_semantics=("arbitrary",),
            vmem_limit_bytes=_vmem_limit(est)),
        name="gla_step",
    )(zq.reshape(n, 1, wq), log_a.reshape(n, 1, GLA_KEY), g_gla.reshape(1, GLA_DV), s0)
    return xb.reshape(n, GLA_VAL), s_new


_PLAN = dict(
    norm=512,
    short_conv=dict(tm=1024, tn=256),
    qkvr=dict(tm=1024, tn=1024),
    log_decay=512,
    merge=dict(tm=512, tn=256),
    mix_out=dict(tm=1024, tn=512),
    ffn_up=dict(tm=1024, tn=256),
    ffn_down=dict(tm=512, tn=512),
    ple=dict(tm=1024, tn=512),
)


def kernel(x_prompt, x_sample, p_prompt, p_sample, state_conv, state_gla, state_ffn, g_mix, w_in,
           w_alpha2, b_alpha, w_conv, w_out_conv, g_gla, w_out_gla, w_mix_out, g_ffn, w_up,
           w_ffn_conv, w_down, g_ple, w_ple_gate, w_ple_proj, g_final):
    assert g_mix.shape[0] == 1, "single-layer trunk"
    assert x_sample.shape[1] == 1, "the sample group advances one token per sequence"
    n_seq, seq_len, d = x_prompt.shape
    n_step = x_sample.shape[0]
    plan = _PLAN
    w_in_t = jnp.swapaxes(w_in[0], 0, 1)
    w_alr_t = w_in_t[OFF_ALR:OFF_GATE_A]
    w_down_b = w_down[0].astype(BF16)
    conv_state, gla_state, ffn_state = state_conv[0], state_gla[0], state_ffn[0]

    h0_q = x_prompt.reshape(n_seq * seq_len, d)
    h0_s = x_sample.reshape(n_step, d)
    p_q = _interleave_rows(p_prompt[0].reshape(n_seq * seq_len, -1).astype(BF16),
                           plan["ple"]["tm"])
    p_s = p_sample[0].reshape(n_step, -1).astype(BF16)

    n1_q = _rmsnorm(h0_q, g_mix[0], BF16, plan["norm"])
    n1_s = _rmsnorm(h0_s, g_mix[0], BF16, plan["norm"])
    zq_q, zq_s = _qkvr_proj(n1_q, n1_s, w_in_t, **plan["qkvr"])
    la_q = _log_decay(n1_q, w_alr_t, w_alpha2[0], b_alpha[0], tm=plan["log_decay"])
    la_s = _log_decay(n1_s, w_alr_t, w_alpha2[0], b_alpha[0], tm=plan["log_decay"])
    (xa_q, conv_last), (xa_s, u_s) = _short_conv_branch(
        n1_q, n1_s, w_in_t, w_conv[0], conv_state.reshape(n_step, -1), seq_len=seq_len,
        **plan["short_conv"])
    xb_q, gla_q = _gla_seq(zq_q, la_q, g_gla[0], n_seq=n_seq, seq_len=seq_len)
    xb_s, gla_s = _gla_step(zq_s, la_s, g_gla[0], gla_state)

    mg_q, mg_s = _merge([n1_q, xa_q, xb_q], [n1_s, xa_s, xb_s], w_in_t, w_out_conv[0],
                        w_out_gla[0], **plan["merge"])
    assert plan["mix_out"]["tm"] == plan["ffn_up"]["tm"] == plan["ple"]["tm"]
    h1_q, h1_s = _proj_residual("mix_out", mg_q, h0_q, mg_s, h0_s, w_mix_out[0],
                                interleave_out=True, **plan["mix_out"])

    n2_q = _rmsnorm(h1_q, g_ffn[0], BF16, plan["norm"])
    n2_s = _rmsnorm(h1_s, g_ffn[0], BF16, plan["norm"])
    (act_q, last_g, last_v), (act_s, ug_s, uv_s) = _conv_ffn_up(
        n2_q, n2_s, w_up[0], w_ffn_conv[0], ffn_state.reshape(n_step, -1), seq_len=seq_len,
        **plan["ffn_up"])
    h2_q, h2_s = _proj_residual("ffn_down", act_q, h1_q, act_s, h1_s, w_down_b,
                                **plan["ffn_down"])

    n3_q = _rmsnorm(h2_q, g_ple[0], BF16, plan["norm"])
    n3_s = _rmsnorm(h2_s, g_ple[0], BF16, plan["norm"])
    h3_q, h3_s = _ple([n3_q, p_q], h2_q, [n3_s, p_s], h2_s, w_ple_gate[0], w_ple_proj[0],
                      **plan["ple"])
    y_q = _rmsnorm(h3_q, g_final, F32, plan["norm"]).reshape(n_seq, seq_len, d)
    y_s = _rmsnorm(h3_s, g_final, F32, plan["norm"]).reshape(n_step, 1, d)

    keep = slice(SUBLANES - (CONV_W - 1), SUBLANES)
    conv_q = conv_last[:, keep, :]
    ffn_q = jnp.concatenate([last_g, last_v], axis=-1)[:, keep, :]
    conv_s = jnp.stack([conv_state[:, 1, :], u_s], axis=1)
    ffn_s = jnp.stack([ffn_state[:, 1, :], jnp.concatenate([ug_s, uv_s], axis=-1)], axis=1)
    return (y_q, y_s, conv_q[None], conv_s[None], gla_q[None], gla_s[None], ffn_q[None],
            ffn_s[None])
```

```python
import functools

import jax
import jax.numpy as jnp
from jax import lax
from jax.experimental import pallas as pl
from jax.experimental.pallas import tpu as pltpu

F32 = jnp.float32
BF16 = jnp.bfloat16

D_MODEL = 4096
D_CONV = 2048
CONV_W = 3
GLA_HEADS = 8
GLA_DK = 256
GLA_DV = 256
GLA_KEY = GLA_HEADS * GLA_DK
GLA_VAL = GLA_HEADS * GLA_DV
GATE_RANK = 16
GATE_NORM = 16.0
GLA_CHUNK = 32
D_FF = 11008
EPS = 1e-6

V7X_VMEM_CAP = 60 * 2**20
SUBLANES = 8
LANES = 128
MXU_DIM = 256
GLA_SUB = MXU_DIM
GLA_STEP_SEQS = 4
CAST_CHUNK_ELEMS = 256 * 1024

OFF_HA = 0
OFF_CA = D_CONV
OFF_BA = 2 * D_CONV
OFF_QKVR = 3 * D_CONV
OFF_ALR = OFF_QKVR + 2 * GLA_KEY + 2 * GLA_VAL
OFF_GATE_A = OFF_ALR + GATE_RANK
OFF_GATE_B = OFF_GATE_A + D_MODEL

_NT = (((1,), (1,)), ((), ()))
_TN = (((0,), (0,)), ((), ()))


def _vmem_limit(nbytes):
    return int(min(V7X_VMEM_CAP, nbytes + 8 * 2**20))


def _nbytes(shape, dtype):
    n = 1
    for s in shape:
        n *= int(s)
    return n * jnp.dtype(dtype).itemsize


def _largest_divisor(n, limit, multiple):
    best = multiple
    for d in range(multiple, min(n, limit) + 1, multiple):
        if n % d == 0:
            best = d
    assert n % best == 0
    return best


def _rmsnorm_kernel(x_ref, g_ref, o_ref):
    x = x_ref[...]
    ms = jnp.mean(x * x, axis=-1, keepdims=True)
    o_ref[...] = ((x * lax.rsqrt(ms + EPS)) * g_ref[...]).astype(o_ref.dtype)


def _rmsnorm(x, g, out_dtype, tm):
    m, d = x.shape
    tm = min(m, tm)
    return pl.pallas_call(
        _rmsnorm_kernel,
        grid=(m // tm,),
        in_specs=[pl.BlockSpec((tm, d), lambda i: (i, 0)),
                  pl.BlockSpec((1, d), lambda i: (0, 0))],
        out_specs=pl.BlockSpec((tm, d), lambda i: (i, 0)),
        out_shape=jax.ShapeDtypeStruct((m, d), out_dtype),
        compiler_params=pltpu.CompilerParams(
            dimension_semantics=("arbitrary",),
            vmem_limit_bytes=_vmem_limit(6 * tm * d * 4)),
        name="rmsnorm",
    )(x, g.reshape(1, d))


def _cast_rows(src_ref, dst_ref):
    rows, cols = src_ref.shape
    ck = _largest_divisor(rows, max(2 * SUBLANES, CAST_CHUNK_ELEMS // cols), 2 * SUBLANES)

    def body(c, carry):
        r = pl.multiple_of(c * ck, ck)
        dst_ref[pl.ds(r, ck), :] = src_ref[pl.ds(r, ck), :].astype(BF16)
        return carry

    lax.fori_loop(0, rows // ck, body, 0)


def _fused_matmul(name, a_list, w_list, extra_list, out_list, epilogue, *, tm, tn, nj,
                  scratch=(), acc_history=False, zero_scratch=False, tail=None, fuse_inputs=()):
    m = a_list[0].shape[0]
    assert m % tm == 0
    ni = m // tm
    t_a, t_extra, t_out, t_epilogue = tail if tail is not None else ([], [], [], None)
    assert len(t_a) in (0, len(a_list))
    na, nw, ne, no = len(a_list), len(w_list), len(extra_list), len(out_list)
    nta, nte, nto = len(t_a), len(t_extra), len(t_out)
    cast_ids = [n for n, w in enumerate(w_list) if w[0].dtype != BF16]

    def w_block(w, off, transposed):
        if transposed:
            assert off % SUBLANES == 0 and tn % SUBLANES == 0
            return ((pl.Element(tn), pl.Element(w.shape[1])),
                    (lambda j, i: (pl.multiple_of(off + j * tn, SUBLANES), 0)))
        assert off % tn == 0
        return (w.shape[0], tn), (lambda j, i: (0, off // tn + j))

    def w_tile_shape(w, transposed):
        return (tn, w.shape[1]) if transposed else (w.shape[0], tn)

    in_specs = [pl.BlockSpec((tm, a.shape[1]), lambda j, i: (i, 0)) for a in a_list]
    in_specs += [pl.BlockSpec(a.shape, lambda j, i: (0, 0)) for a in t_a]
    in_specs += [pl.BlockSpec(*w_block(w, off, tr)) for w, _, off, tr in w_list]
    in_specs += [pl.BlockSpec(blk, im) for _, blk, im in list(extra_list) + list(t_extra)]
    out_specs = [pl.BlockSpec(blk, im) for _, blk, im in list(out_list) + list(t_out)]
    out_shape = [s for s, _, _ in list(out_list) + list(t_out)]
    scratch_shapes = [pltpu.VMEM(w_tile_shape(w_list[n][0], w_list[n][3]), BF16) for n in cast_ids]
    n_acc = nw if acc_history else 0
    scratch_shapes += [pltpu.VMEM((2 * SUBLANES + tm, tn), F32)] * n_acc
    scratch_shapes += list(scratch)

    def kernel(*refs):
        pos = 0

        def take(count):
            nonlocal pos
            pos += count
            return refs[pos - count:pos]

        a_refs, ta_refs, w_refs = take(na), take(nta), take(nw)
        e_refs, te_refs, o_refs, to_refs = take(ne), take(nte), take(no), take(nto)
        s_refs = refs[pos:]
        wb_refs = dict(zip(cast_ids, s_refs[:len(cast_ids)]))
        acc_refs = s_refs[len(cast_ids):len(cast_ids) + n_acc]
        user_scratch = s_refs[len(cast_ids) + n_acc:]
        j = pl.program_id(0)
        i = pl.program_id(1)

        @pl.when(i == 0)
        def _():
            for n in cast_ids:
                _cast_rows(w_refs[n], wb_refs[n])
            for ref in list(acc_refs) + (list(user_scratch) if zero_scratch else []):
                ref[...] = jnp.zeros_like(ref)

        def dots(lhs_refs):
            accs = []
            for n, (_, ai, _, transposed) in enumerate(w_list):
                w_ref = wb_refs[n] if n in wb_refs else w_refs[n]
                a = lhs_refs[ai][...]
                if transposed:
                    accs.append(lax.dot_general(a, w_ref[...], _NT, preferred_element_type=F32))
                else:
                    accs.append(jnp.dot(a, w_ref[...], preferred_element_type=F32))
            return accs

        accs = dots(a_refs)
        if acc_history:
            for n in range(nw):
                _push_interleaved(acc_refs[n], accs[n])
            accs = list(acc_refs)
        epilogue(accs, e_refs, o_refs, user_scratch, j, i)

        if t_epilogue is not None:
            @pl.when(i == ni - 1)
            def _():
                t_epilogue(dots(ta_refs), te_refs, to_refs, j)

    est = 0
    for a in a_list:
        est += 2 * _nbytes((tm, a.shape[1]), a.dtype)
    for a in t_a:
        est += 2 * _nbytes(a.shape, a.dtype)
    for w, _, _, tr in w_list:
        est += 2 * _nbytes(w_tile_shape(w, tr), w.dtype)
        if w.dtype != BF16:
            est += _nbytes(w_tile_shape(w, tr), BF16)
    for arr, blk, _ in list(extra_list) + list(out_list) + list(t_extra) + list(t_out):
        est += 2 * _nbytes(blk, arr.dtype)
    est += 3 * nw * tm * tn * 4

    outs = pl.pallas_call(
        kernel,
        grid=(nj, ni),
        in_specs=in_specs,
        out_specs=out_specs,
        out_shape=out_shape,
        scratch_shapes=scratch_shapes,
        compiler_params=pltpu.CompilerParams(
            dimension_semantics=("arbitrary", "arbitrary"),
            vmem_limit_bytes=_vmem_limit(est),
            allow_input_fusion=[n in fuse_inputs for n in range(na + nta + nw + ne + nte)]),
        name=name,
    )(*a_list, *t_a, *[w[0] for w in w_list], *[e for e, _, _ in extra_list],
      *[e for e, _, _ in t_extra])
    return outs[:no], outs[no:]


def _tile(tm, tn):
    return (tm, tn), (lambda j, i: (i, j))


def _tail_tile(rows, tn, off_blocks=0):
    return (rows, tn), (lambda j, i: (0, off_blocks + j))


def _push_history(hist_ref, tile):
    prev_tail = _history_tail(hist_ref)
    hist_ref[SUBLANES:SUBLANES + tile.shape[0], :] = tile
    hist_ref[0:SUBLANES, :] = prev_tail


def _conv3_history(hist_ref, seq_start, w3):
    tm = hist_ref.shape[0] - SUBLANES
    hist_ref[0:SUBLANES, :] = jnp.where(seq_start, 0.0, hist_ref[0:SUBLANES, :])
    taps = [hist_ref[SUBLANES - (CONV_W - 1) + t:SUBLANES - (CONV_W - 1) + t + tm, :]
            for t in range(CONV_W)]
    return w3[0:1, :] * taps[0] + w3[1:2, :] * taps[1] + w3[2:3, :] * taps[2]


def _history_tail(hist_ref):
    tm = hist_ref.shape[0] - SUBLANES
    return hist_ref[tm:tm + SUBLANES, :]


def _stage_shape(tm, tn):
    return (tn // LANES, tm, LANES)


def _store_interleaved(o_ref, stage_ref, natural):
    nv = natural.shape[0] // SUBLANES
    for c in range(stage_ref.shape[0]):
        lanes = slice(c * LANES, (c + 1) * LANES)
        for s in range(SUBLANES):
            stage_ref[c, pl.ds(s, nv, stride=SUBLANES), :] = natural[s * nv:(s + 1) * nv, lanes]
        o_ref[:, lanes] = stage_ref[c]


def _store_natural(o_ref, stage_ref, interleaved):
    nv = interleaved.shape[0] // SUBLANES
    for c in range(stage_ref.shape[0]):
        lanes = slice(c * LANES, (c + 1) * LANES)
        stage_ref[c] = interleaved[:, lanes]
        for s in range(SUBLANES):
            o_ref[s * nv:(s + 1) * nv, lanes] = stage_ref[c, pl.ds(s, nv, stride=SUBLANES), :]


def _interleave_rows(x, group):
    m, n = x.shape
    nv = group // SUBLANES
    return x.reshape(m // group, SUBLANES, nv, n).swapaxes(1, 2).reshape(m, n)


def _push_interleaved(hist_ref, tile):
    tm = tile.shape[0]
    two = 2 * SUBLANES
    prev_last = hist_ref[tm:tm + two, :]
    hist_ref[two:two + tm, :] = tile
    row = lax.broadcasted_iota(jnp.int32, (SUBLANES, tile.shape[1]), 0)
    for k in range(2):
        cur = tile[tm - two + SUBLANES * k:tm - SUBLANES + SUBLANES * k, :]
        carry = prev_last[SUBLANES * k + SUBLANES - 1:SUBLANES * (k + 1), :]
        hist_ref[SUBLANES * k:SUBLANES * (k + 1), :] = jnp.where(
            row == 0, carry, pltpu.roll(cur, 1, axis=0))


def _conv3_interleaved(hist_ref, seq_start, w3):
    two = 2 * SUBLANES
    tm = hist_ref.shape[0] - two
    row = lax.broadcasted_iota(jnp.int32, (two, hist_ref.shape[1]), 0)
    hist_ref[0:two, :] = jnp.where(seq_start & (row % SUBLANES == 0), 0.0, hist_ref[0:two, :])
    taps = [hist_ref[SUBLANES * t:SUBLANES * t + tm, :] for t in range(CONV_W)]
    return w3[0:1, :] * taps[0] + w3[1:2, :] * taps[1] + w3[2:3, :] * taps[2]


def _last_two_tokens(hist_ref):
    tm = hist_ref.shape[0] - 2 * SUBLANES
    u_a = hist_ref[tm:tm + SUBLANES, :]
    u_b = hist_ref[tm + SUBLANES:tm + 2 * SUBLANES, :]
    row = lax.broadcasted_iota(jnp.int32, u_a.shape, 0)
    return jnp.where(row == SUBLANES - 2, pltpu.roll(u_a, SUBLANES - 1, axis=0), u_b)


def _conv3_step(w3, prev2, prev1, cur):
    return w3[0:1, :] * prev2 + w3[1:2, :] * prev1 + w3[2:3, :] * cur


def _silu(x):
    return x * jax.nn.sigmoid(x)


def _short_conv_branch(n1_q, n1_s, w_in_t, w_conv, state2d, *, seq_len, tm, tn):
    m, ms = n1_q.shape[0], n1_s.shape[0]
    nj = D_CONV // tn
    tps = seq_len // tm
    n_seq = m // seq_len
    w_list = [(w_in_t, 0, OFF_HA, True), (w_in_t, 0, OFF_CA, True), (w_in_t, 0, OFF_BA, True)]
    wc = (w_conv, (CONV_W, tn), lambda j, i: (0, j))

    def epilogue(accs, e_refs, o_refs, s_refs, j, i):
        h, c, b = accs
        (wc_ref,), (xa_ref, last_ref), (u_hist,) = e_refs, o_refs, s_refs
        _push_history(u_hist, c * h)
        cu = _conv3_history(u_hist, i % tps == 0, wc_ref[...])
        xa_ref[...] = (b * cu).astype(BF16)
        last_ref[0] = _history_tail(u_hist)

    def step_epilogue(accs, e_refs, o_refs, j):
        h, c, b = accs
        wc_ref, s0_ref, s1_ref = e_refs
        xa_ref, u_ref = o_refs
        u = c * h
        xa_ref[...] = (b * _conv3_step(wc_ref[...], s0_ref[...], s1_ref[...], u)).astype(BF16)
        u_ref[...] = u

    outs = [(jax.ShapeDtypeStruct((m, D_CONV), BF16),) + _tile(tm, tn),
            (jax.ShapeDtypeStruct((n_seq, SUBLANES, D_CONV), F32), (1, SUBLANES, tn),
             lambda j, i: (i // tps, 0, j))]
    tail = ([n1_s],
            [wc, (state2d,) + _tail_tile(ms, tn), (state2d,) + _tail_tile(ms, tn, nj)],
            [(jax.ShapeDtypeStruct((ms, D_CONV), BF16),) + _tail_tile(ms, tn),
             (jax.ShapeDtypeStruct((ms, D_CONV), F32),) + _tail_tile(ms, tn)],
            step_epilogue)
    return _fused_matmul("short_conv", [n1_q], w_list, [wc], outs, epilogue,
                         tm=tm, tn=tn, nj=nj, scratch=[pltpu.VMEM((SUBLANES + tm, tn), F32)],
                         zero_scratch=True, tail=tail)


def _qkvr_proj(n1_q, n1_s, w_in_t, *, tm, tn):
    m, ms = n1_q.shape[0], n1_s.shape[0]
    n = 2 * GLA_KEY + 2 * GLA_VAL

    def epilogue(accs, e_refs, o_refs, s_refs, j, i):
        o_refs[0][...] = accs[0]

    def step_epilogue(accs, e_refs, o_refs, j):
        o_refs[0][...] = accs[0]

    outs = [(jax.ShapeDtypeStruct((m, n), F32),) + _tile(tm, tn)]
    tail = ([n1_s], [], [(jax.ShapeDtypeStruct((ms, n), F32),) + _tail_tile(ms, tn)],
            step_epilogue)
    (zq,), (zq_s,) = _fused_matmul("qkvr_proj", [n1_q], [(w_in_t, 0, OFF_QKVR, True)], [], outs,
                                   epilogue, tm=tm, tn=tn, nj=n // tn, tail=tail)
    return zq, zq_s


def _log_decay_kernel(n_ref, w1_ref, w2_ref, b_ref, o_ref):
    a_lr = lax.dot_general(n_ref[...], w1_ref[...], _NT, preferred_element_type=F32)
    x = jnp.dot(a_lr.astype(BF16), w2_ref[...], preferred_element_type=F32) + b_ref[...]
    log_sig = jnp.minimum(x, 0.0) - jnp.log1p(jnp.exp(-jnp.abs(x)))
    o_ref[...] = log_sig / GATE_NORM


def _log_decay(n1, w_alr_t, w_alpha2, b_alpha, *, tm):
    m, d = n1.shape
    tm = min(m, tm)
    w1 = jnp.pad(w_alr_t, ((0, LANES - GATE_RANK), (0, 0))).astype(BF16)
    w2 = jnp.pad(w_alpha2, ((0, LANES - GATE_RANK), (0, 0))).astype(BF16)
    return pl.pallas_call(
        _log_decay_kernel,
        grid=(m // tm,),
        in_specs=[pl.BlockSpec((tm, d), lambda i: (i, 0)),
                  pl.BlockSpec((LANES, d), lambda i: (0, 0)),
                  pl.BlockSpec((LANES, GLA_KEY), lambda i: (0, 0)),
                  pl.BlockSpec((1, GLA_KEY), lambda i: (0, 0))],
        out_specs=pl.BlockSpec((tm, GLA_KEY), lambda i: (i, 0)),
        out_shape=jax.ShapeDtypeStruct((m, GLA_KEY), F32),
        compiler_params=pltpu.CompilerParams(
            dimension_semantics=("arbitrary",),
            vmem_limit_bytes=_vmem_limit(2 * tm * d * 2 + 6 * tm * GLA_KEY * 4 + 4 * 2**20)),
        name="log_decay",
    )(n1, w1, w2, b_alpha.reshape(1, GLA_KEY))


def _merge(lhs_q, lhs_s, w_in_t, w_out_conv, w_out_gla, *, tm, tn):
    m, ms = lhs_q[0].shape[0], lhs_s[0].shape[0]

    def merged(accs):
        ga, gb, ya, yb = accs
        return (jax.nn.sigmoid(ga) * ya + jax.nn.sigmoid(gb) * yb).astype(BF16)

    def epilogue(accs, e_refs, o_refs, s_refs, j, i):
        o_refs[0][...] = merged(accs)

    def step_epilogue(accs, e_refs, o_refs, j):
        o_refs[0][...] = merged(accs)

    w_list = [(w_in_t, 0, OFF_GATE_A, True), (w_in_t, 0, OFF_GATE_B, True),
              (w_out_conv, 1, 0, False), (w_out_gla, 2, 0, False)]
    outs = [(jax.ShapeDtypeStruct((m, D_MODEL), BF16),) + _tile(tm, tn)]
    tail = (list(lhs_s), [], [(jax.ShapeDtypeStruct((ms, D_MODEL), BF16),) + _tail_tile(ms, tn)],
            step_epilogue)
    (out,), (out_s,) = _fused_matmul("gated_merge", list(lhs_q), w_list, [], outs, epilogue,
                                     tm=tm, tn=tn, nj=D_MODEL // tn, tail=tail)
    return out, out_s


def _proj_residual(name, a_q, res_q, a_s, res_s, w, *, tm, tn, interleave_out=False):
    m, ms = a_q.shape[0], a_s.shape[0]
    n = w.shape[1]

    def epilogue(accs, e_refs, o_refs, s_refs, j, i):
        if interleave_out:
            _store_interleaved(o_refs[0], s_refs[0], e_refs[0][...] + accs[0])
        else:
            o_refs[0][...] = e_refs[0][...] + accs[0]

    def step_epilogue(accs, e_refs, o_refs, j):
        o_refs[0][...] = e_refs[0][...] + accs[0]

    extras = [(res_q,) + _tile(tm, tn)]
    outs = [(jax.ShapeDtypeStruct((m, n), F32),) + _tile(tm, tn)]
    tail = ([a_s], [(res_s,) + _tail_tile(ms, tn)],
            [(jax.ShapeDtypeStruct((ms, n), F32),) + _tail_tile(ms, tn)], step_epilogue)
    scratch = [pltpu.VMEM(_stage_shape(tm, tn), F32)] if interleave_out else []
    (out,), (out_s,) = _fused_matmul(name, [a_q], [(w, 0, 0, False)], extras, outs, epilogue,
                                     tm=tm, tn=tn, nj=n // tn, tail=tail, scratch=scratch,
                                     fuse_inputs=(2,) if w.dtype == BF16 else ())
    return out, out_s


def _conv_ffn_up(n2_q, n2_s, w_up, w_ffn_conv, state2d, *, seq_len, tm, tn):
    m, ms = n2_q.shape[0], n2_s.shape[0]
    nj = D_FF // tn
    tps = seq_len // tm
    n_seq = m // seq_len
    w_list = [(w_up, 0, 0, False), (w_up, 0, D_FF, False)]
    wcg = (w_ffn_conv, (CONV_W, tn), lambda j, i: (0, j))
    wcv = (w_ffn_conv, (CONV_W, tn), lambda j, i: (0, nj + j))

    def epilogue(accs, e_refs, o_refs, s_refs, j, i):
        ug_hist, uv_hist = accs
        wg_ref, wv_ref = e_refs
        act_ref, lastg_ref, lastv_ref = o_refs
        seq_start = i % tps == 0
        cg = _conv3_interleaved(ug_hist, seq_start, wg_ref[...])
        cv = _conv3_interleaved(uv_hist, seq_start, wv_ref[...])
        act_ref[...] = (_silu(cg) * cv).astype(BF16)
        lastg_ref[0] = _last_two_tokens(ug_hist)
        lastv_ref[0] = _last_two_tokens(uv_hist)

    def step_epilogue(accs, e_refs, o_refs, j):
        ug, uv = accs
        wg_ref, wv_ref, g0, v0, g1, v1 = e_refs
        act_ref, ug_ref, uv_ref = o_refs
        cg = _conv3_step(wg_ref[...], g0[...], g1[...], ug)
        cv = _conv3_step(wv_ref[...], v0[...], v1[...], uv)
        act_ref[...] = (_silu(cg) * cv).astype(BF16)
        ug_ref[...] = ug
        uv_ref[...] = uv

    last = (jax.ShapeDtypeStruct((n_seq, SUBLANES, D_FF), F32), (1, SUBLANES, tn),
            lambda j, i: (i // tps, 0, j))
    outs = [(jax.ShapeDtypeStruct((m, D_FF), BF16),) + _tile(tm, tn), last, last]
    up_s = (jax.ShapeDtypeStruct((ms, D_FF), F32),) + _tail_tile(ms, tn)
    tail = ([n2_s],
            [wcg, wcv] + [(state2d,) + _tail_tile(ms, tn, q * nj) for q in range(4)],
            [(jax.ShapeDtypeStruct((ms, D_FF), BF16),) + _tail_tile(ms, tn), up_s, up_s],
            step_epilogue)
    return _fused_matmul("conv_ffn_up", [n2_q], w_list, [wcg, wcv], outs, epilogue,
                         tm=tm, tn=tn, nj=nj, acc_history=True, tail=tail)


def _ple(lhs_q, h2_q, lhs_s, h2_s, w_ple_gate, w_ple_proj, *, tm, tn):
    m, ms = h2_q.shape[0], h2_s.shape[0]

    def epilogue(accs, e_refs, o_refs, s_refs, j, i):
        gate, proj = accs
        _store_natural(o_refs[0], s_refs[0], e_refs[0][...] + jax.nn.sigmoid(gate) * proj)

    def step_epilogue(accs, e_refs, o_refs, j):
        gate, proj = accs
        o_refs[0][...] = e_refs[0][...] + jax.nn.sigmoid(gate) * proj

    extras = [(h2_q,) + _tile(tm, tn)]
    outs = [(jax.ShapeDtypeStruct((m, D_MODEL), F32),) + _tile(tm, tn)]
    tail = (list(lhs_s), [(h2_s,) + _tail_tile(ms, tn)],
            [(jax.ShapeDtypeStruct((ms, D_MODEL), F32),) + _tail_tile(ms, tn)], step_epilogue)
    (out,), (out_s,) = _fused_matmul(
        "ple_gate", list(lhs_q), [(w_ple_gate, 0, 0, False), (w_ple_proj, 1, 0, False)], extras,
        outs, epilogue, tm=tm, tn=tn, nj=D_MODEL // tn, tail=tail,
        scratch=[pltpu.VMEM(_stage_shape(tm, tn), F32)], fuse_inputs=(1, 3))
    return out, out_s


def _gla_seq_kernel(q_ref, k_ref, v_ref, r_ref, la_ref, g_ref, xb_ref, sfin_ref,
                    st_ref, b_sc, cum_sc, *, seq_len):
    sub, c = GLA_SUB, GLA_CHUNK
    n_chunks = sub // c
    st_ref[...] = jnp.zeros_like(st_ref)
    row = lax.broadcasted_iota(jnp.int32, (sub, sub), 0)
    col = lax.broadcasted_iota(jnp.int32, (sub, sub), 1)
    row_in_chunk = row % c
    groups = [g for g in (2 * c, 4 * c, 8 * c) if g <= sub]

    def sub_block(sb, carry):
        r0 = pl.multiple_of(sb * sub, sub)
        b = la_ref[pl.ds(r0, sub), :]
        s = 1
        while s < c:
            b = b + jnp.where(row_in_chunk >= s, pltpu.roll(b, s, axis=0), 0.0)
            s *= 2
        b_sc[...] = b
        offset = jnp.zeros((1, GLA_DK), F32)
        for n in range(n_chunks):
            cum_sc[n * c:(n + 1) * c, :] = b[n * c:(n + 1) * c, :] + offset
            offset = offset + b_sc[n * c + c - 1:n * c + c, :]
        cum = cum_sc[...]
        cum_last = offset

        q = q_ref[pl.ds(r0, sub), :] * (GLA_DK ** -0.5)
        k = k_ref[pl.ds(r0, sub), :]
        vb = v_ref[pl.ds(r0, sub), :].astype(BF16)

        qt = (q * jnp.exp(b)).astype(BF16)
        kt = (k * jnp.exp(-b)).astype(BF16)
        diag = ((row // c) == (col // c)) & (col <= row)
        scores = jnp.where(diag, lax.dot_general(qt, kt, _NT, preferred_element_type=F32), 0.0)
        for g in groups:
            half = g // 2
            mid = jnp.concatenate(
                [jnp.broadcast_to(cum_sc[grp * g + half - 1:grp * g + half, :], (g, GLA_DK))
                 for grp in range(sub // g)], axis=0)
            factor = jnp.exp(jnp.where(row % g >= half, cum - mid, mid - cum))
            cross = lax.dot_general((q * factor).astype(BF16), (k * factor).astype(BF16), _NT,
                                    preferred_element_type=F32)
            pair = ((row // g) == (col // g)) & (row % g >= half) & (col % g < half)
            scores = jnp.where(pair, cross, scores)
        o = jnp.dot(scores.astype(BF16), vb, preferred_element_type=F32)

        st = st_ref[...]
        o = o + lax.dot_general((q * jnp.exp(cum)).astype(BF16), st.astype(BF16), _NT,
                                preferred_element_type=F32)
        k_end = (k * jnp.exp(cum_last - cum)).astype(BF16)
        st_ref[...] = st * jnp.exp(cum_last) + lax.dot_general(vb, k_end, _TN,
                                                                preferred_element_type=F32)
        ms = jnp.mean(o * o, axis=-1, keepdims=True)
        y = (o * lax.rsqrt(ms + EPS)) * g_ref[...]
        r = r_ref[pl.ds(r0, sub), :]
        xb_ref[pl.ds(r0, sub), :] = (y * _silu(r)).astype(BF16)
        return carry

    lax.fori_loop(0, seq_len // sub, sub_block, 0)
    sfin_ref[0, 0] = st_ref[...].T


def _gla_seq(zq, log_a, g_gla, *, n_seq, seq_len):
    h = GLA_HEADS
    blk = (seq_len, GLA_DK)

    def col(off):
        return pl.BlockSpec(blk, functools.partial(lambda b, hh, off: (b, off + hh), off=off))

    est = 2 * 5 * _nbytes(blk, F32) + 2 * _nbytes(blk, BF16) + 16 * GLA_SUB * GLA_SUB * 4
    return pl.pallas_call(
        functools.partial(_gla_seq_kernel, seq_len=seq_len),
        grid=(n_seq, h),
        in_specs=[col(0), col(h), col(2 * h), col(3 * h),
                  pl.BlockSpec(blk, lambda b, hh: (b, hh)),
                  pl.BlockSpec((1, GLA_DV), lambda b, hh: (0, 0))],
        out_specs=[pl.BlockSpec(blk, lambda b, hh: (b, hh)),
                   pl.BlockSpec((1, 1, GLA_DK, GLA_DV), lambda b, hh: (b, hh, 0, 0))],
        out_shape=[jax.ShapeDtypeStruct((n_seq * seq_len, GLA_VAL), BF16),
                   jax.ShapeDtypeStruct((n_seq, h, GLA_DK, GLA_DV), F32)],
        scratch_shapes=[pltpu.VMEM((GLA_DV, GLA_DK), F32),
                        pltpu.VMEM((GLA_SUB, GLA_DK), F32),
                        pltpu.VMEM((GLA_SUB, GLA_DK), F32)],
        compiler_params=pltpu.CompilerParams(
            dimension_semantics=("arbitrary", "arbitrary"),
            vmem_limit_bytes=_vmem_limit(est)),
        name="gla_seq",
    )(zq, zq, zq, zq, log_a, g_gla.reshape(1, GLA_DV))


def _to_column(x_row, eye):
    return jnp.sum(jnp.where(eye, x_row, 0.0), axis=1, keepdims=True)


def _gla_step_kernel(zq_ref, la_ref, g_ref, s0_ref, xb_ref, snew_ref):
    eye = (lax.broadcasted_iota(jnp.int32, (GLA_DK, GLA_DK), 0)
           == lax.broadcasted_iota(jnp.int32, (GLA_DK, GLA_DK), 1))
    pad_rows = 2 * SUBLANES

    def one_sequence(s, carry):
        for h in range(GLA_HEADS):
            def cols(base, width=GLA_DK, h=h):
                return slice(base + h * width, base + (h + 1) * width)
            q = zq_ref[s, :, cols(0)] * (GLA_DK ** -0.5)
            k = zq_ref[s, :, cols(GLA_KEY)]
            v = zq_ref[s, :, cols(2 * GLA_KEY)]
            r = zq_ref[s, :, cols(2 * GLA_KEY + GLA_VAL)]
            b = la_ref[s, :, cols(0)]
            qt = q * jnp.exp(b)
            kt = k * jnp.exp(-b)
            a = jnp.exp(b)
            s0 = s0_ref[s, h]
            score = jnp.sum(qt * kt, axis=-1, keepdims=True)
            o_inter = jnp.dot(jnp.broadcast_to(qt, (pad_rows, GLA_DK)).astype(BF16),
                              s0.astype(BF16), preferred_element_type=F32)[0:1, :]
            o = score * v + o_inter
            snew_ref[s, h] = s0 * _to_column(a, eye) + _to_column(k, eye) * v
            ms = jnp.mean(o * o, axis=-1, keepdims=True)
            y = (o * lax.rsqrt(ms + EPS)) * g_ref[...]
            xb_ref[s, :, cols(0, GLA_DV)] = (y * _silu(r)).astype(BF16)
        return carry

    lax.fori_loop(0, zq_ref.shape[0], one_sequence, 0)


def _gla_step(zq, log_a, g_gla, s0):
    n = zq.shape[0]
    wq = zq.shape[1]
    per_step = GLA_STEP_SEQS if n % GLA_STEP_SEQS == 0 else 1
    sblk = (per_step, GLA_HEADS, GLA_DK, GLA_DV)
    est = 4 * _nbytes(sblk, F32) + 8 * GLA_DK * GLA_DV * 4
    xb, s_new = pl.pallas_call(
        _gla_step_kernel,
        grid=(n // per_step,),
        in_specs=[pl.BlockSpec((per_step, 1, wq), lambda b: (b, 0, 0)),
                  pl.BlockSpec((per_step, 1, GLA_KEY), lambda b: (b, 0, 0)),
                  pl.BlockSpec((1, GLA_DV), lambda b: (0, 0)),
                  pl.BlockSpec(sblk, lambda b: (b, 0, 0, 0))],
        out_specs=[pl.BlockSpec((per_step, 1, GLA_VAL), lambda b: (b, 0, 0)),
                   pl.BlockSpec(sblk, lambda b: (b, 0, 0, 0))],
        out_shape=[jax.ShapeDtypeStruct((n, 1, GLA_VAL), BF16),
                   jax.ShapeDtypeStruct(s0.shape, F32)],
        compiler_params=pltpu.CompilerParams(
            dimension_semantics=("arbitrary",),
            vmem_limit_bytes=_vmem_limit(est)),
        name="gla_step",
    )(zq.reshape(n, 1, wq), log_a.reshape(n, 1, GLA_KEY), g_gla.reshape(1, GLA_DV), s0)
    return xb.reshape(n, GLA_VAL), s_new


_PLAN = dict(
    norm=512,
    short_conv=dict(tm=1024, tn=256),
    qkvr=dict(tm=1024, tn=512),
    log_decay=512,
    merge=dict(tm=512, tn=256),
    mix_out=dict(tm=1024, tn=512),
    ffn_up=dict(tm=1024, tn=256),
    ffn_down=dict(tm=512, tn=512),
    ple=dict(tm=1024, tn=512),
)


def kernel(x_prompt, x_sample, p_prompt, p_sample, state_conv, state_gla, state_ffn, g_mix, w_in,
           w_alpha2, b_alpha, w_conv, w_out_conv, g_gla, w_out_gla, w_mix_out, g_ffn, w_up,
           w_ffn_conv, w_down, g_ple, w_ple_gate, w_ple_proj, g_final):
    assert g_mix.shape[0] == 1, "single-layer trunk"
    assert x_sample.shape[1] == 1, "the sample group advances one token per sequence"
    n_seq, seq_len, d = x_prompt.shape
    n_step = x_sample.shape[0]
    plan = _PLAN
    w_in_t = jnp.swapaxes(w_in[0], 0, 1)
    w_alr_t = w_in_t[OFF_ALR:OFF_GATE_A]
    w_down_b = w_down[0].astype(BF16)
    conv_state, gla_state, ffn_state = state_conv[0], state_gla[0], state_ffn[0]

    h0_q = x_prompt.reshape(n_seq * seq_len, d)
    h0_s = x_sample.reshape(n_step, d)
    p_q = _interleave_rows(p_prompt[0].reshape(n_seq * seq_len, -1).astype(BF16),
                           plan["ple"]["tm"])
    p_s = p_sample[0].reshape(n_step, -1).astype(BF16)

    n1_q = _rmsnorm(h0_q, g_mix[0], BF16, plan["norm"])
    n1_s = _rmsnorm(h0_s, g_mix[0], BF16, plan["norm"])
    zq_q, zq_s = _qkvr_proj(n1_q, n1_s, w_in_t, **plan["qkvr"])
    la_q = _log_decay(n1_q, w_alr_t, w_alpha2[0], b_alpha[0], tm=plan["log_decay"])
    la_s = _log_decay(n1_s, w_alr_t, w_alpha2[0], b_alpha[0], tm=plan["log_decay"])
    (xa_q, conv_last), (xa_s, u_s) = _short_conv_branch(
        n1_q, n1_s, w_in_t, w_conv[0], conv_state.reshape(n_step, -1), seq_len=seq_len,
        **plan["short_conv"])
    xb_q, gla_q = _gla_seq(zq_q, la_q, g_gla[0], n_seq=n_seq, seq_len=seq_len)
    xb_s, gla_s = _gla_step(zq_s, la_s, g_gla[0], gla_state)

    mg_q, mg_s = _merge([n1_q, xa_q, xb_q], [n1_s, xa_s, xb_s], w_in_t, w_out_conv[0],
                        w_out_gla[0], **plan["merge"])
    assert plan["mix_out"]["tm"] == plan["ffn_up"]["tm"] == plan["ple"]["tm"]
    h1_q, h1_s = _proj_residual("mix_out", mg_q, h0_q, mg_s, h0_s, w_mix_out[0],
                                interleave_out=True, **plan["mix_out"])

    n2_q = _rmsnorm(h1_q, g_ffn[0], BF16, plan["norm"])
    n2_s = _rmsnorm(h1_s, g_ffn[0], BF16, plan["norm"])
    (act_q, last_g, last_v), (act_s, ug_s, uv_s) = _conv_ffn_up(
        n2_q, n2_s, w_up[0], w_ffn_conv[0], ffn_state.reshape(n_step, -1), seq_len=seq_len,
        **plan["ffn_up"])
    h2_q, h2_s = _proj_residual("ffn_down", act_q, h1_q, act_s, h1_s, w_down_b,
                                **plan["ffn_down"])

    n3_q = _rmsnorm(h2_q, g_ple[0], BF16, plan["norm"])
    n3_s = _rmsnorm(h2_s, g_ple[0], BF16, plan["norm"])
    h3_q, h3_s = _ple([n3_q, p_q], h2_q, [n3_s, p_s], h2_s, w_ple_gate[0], w_ple_proj[0],
                      **plan["ple"])
    y_q = _rmsnorm(h3_q, g_final, F32, plan["norm"]).reshape(n_seq, seq_len, d)
    y_s = _rmsnorm(h3_s, g_final, F32, plan["norm"]).reshape(n_step, 1, d)

    keep = slice(SUBLANES - (CONV_W - 1), SUBLANES)
    conv_q = conv_last[:, keep, :]
    ffn_q = jnp.concatenate([last_g, last_v], axis=-1)[:, keep, :]
    conv_s = jnp.stack([conv_state[:, 1, :], u_s], axis=1)
    ffn_s = jnp.stack([ffn_state[:, 1, :], jnp.concatenate([ug_s, uv_s], axis=-1)], axis=1)
    return (y_q, y_s, conv_q[None], conv_s[None], gla_q[None], gla_s[None], ffn_q[None],
            ffn_s[None])
```
